```python
import math
import jax, jax.numpy as jnp
from jax import lax
import numpy as np

D_MODEL = 4096
BATCH = 1
SEQ = 8192
DEPTH = 4

GRID_W = 64
CTX_LEN = 256
CHUNK = 128
BRANCH_WIDTH = 1024
N_BRANCH = 3
A_GROUPS = 8
A_GDIM = BRANCH_WIDTH // A_GROUPS
B_HEADS = 4
B_DQK = 128
B_DV = BRANCH_WIDTH // B_HEADS
C_HEADS = 4
C_DH = 128
C_DV = 2 * C_DH
FFN_DIM = 4096
ADA_RANK = 256
N_SUB = 3
ROPE_THETA = 10000.0
EPS = 1e-6
SPLIT_SIZES = (2 * BRANCH_WIDTH,
               B_HEADS * B_DQK,
               B_HEADS * B_DQK,
               B_HEADS * B_DV,
               B_HEADS * B_DV,
               4 * B_HEADS,
               2 * C_HEADS * C_DH,
               2 * C_HEADS * C_DH,
               C_HEADS * C_DV,
               N_BRANCH * D_MODEL)
SPLIT_POINTS = tuple(sum(SPLIT_SIZES[:i + 1]) for i in range(len(SPLIT_SIZES) - 1))
IN_COLS = sum(SPLIT_SIZES)

kernel_name = 'hybrid_gated_mixer_dit'


def rmsnorm(x, g):
    xf = x.astype(jnp.float32)
    y = xf * lax.rsqrt(jnp.mean(xf * xf, axis=-1, keepdims=True) + EPS)
    return (y * g.astype(jnp.float32)).astype(x.dtype)


def layernorm(x, g):
    xf = x.astype(jnp.float32)
    xc = xf - jnp.mean(xf, axis=-1, keepdims=True)
    y = xc * lax.rsqrt(jnp.mean(xc * xc, axis=-1, keepdims=True) + EPS)
    return (y * g.astype(jnp.float32)).astype(x.dtype)


def modulation(cvec, w_down, w_up, b_up):
    m = (jax.nn.silu(cvec) @ w_down) @ w_up + b_up
    return m.reshape(m.shape[:-1] + (N_SUB, 3, D_MODEL))


def modulate(x, g, shift, scale):
    return rmsnorm(x, g) * (1 + scale) + shift


def swiglu(h, w13, w2):
    a, b = jnp.split(h @ w13, 2, axis=-1)
    return (jax.nn.silu(a) * b) @ w2


def ffn_sublayer(x, g, mod, w13, w2):
    h = modulate(x, g, mod[:, :, 0], mod[:, :, 1])
    return x + 0.5 * mod[:, :, 2] * swiglu(h, w13, w2)


def axial_rope(rows, dim):
    r = jnp.repeat(jnp.arange(rows, dtype=jnp.float32), GRID_W)
    col = jnp.tile(jnp.arange(GRID_W, dtype=jnp.float32), rows)
    n_freq = dim // 4
    inv = ROPE_THETA ** (-jnp.arange(n_freq, dtype=jnp.float32) / n_freq)
    ang = jnp.concatenate([r[:, None] * inv, col[:, None] * inv], axis=-1)
    return jnp.cos(ang), jnp.sin(ang)


def apply_rope(x, cos, sin):
    shp = (cos.shape[0],) + (1,) * (x.ndim - 3) + (cos.shape[1],)
    c = cos.reshape(shp)
    s = sin.reshape(shp)
    xp = x.astype(jnp.float32).reshape(x.shape[:-1] + (x.shape[-1] // 2, 2))
    x1, x2 = xp[..., 0], xp[..., 1]
    out = jnp.stack([x1 * c - x2 * s, x1 * s + x2 * c], axis=-1)
    return out.reshape(x.shape).astype(x.dtype)


def gmlp_branch(uv, norm_g, ws, bs):
    u, v = jnp.split(jax.nn.gelu(uv), 2, axis=-1)
    v = layernorm(v, norm_g)
    Bn, T, _ = v.shape
    v = v.reshape(Bn, T // CHUNK, CHUNK, A_GROUPS, A_GDIM)
    mixed = jnp.einsum('gpq,bnqgc->bnpgc', ws, v) + bs.T[:, :, None]
    return u * mixed.reshape(Bn, T, BRANCH_WIDTH)


def mlstm_scan(q, k, v, i_pre, f_pre, state):
    Bn, H, T, _ = q.shape
    nc = T // CHUNK

    def chunks(a):
        a = a.astype(jnp.float32)
        return jnp.moveaxis(a.reshape((Bn, H, nc, CHUNK) + a.shape[3:]), 2, 0)

    tri = jnp.tril(jnp.ones((CHUNK, CHUNK), dtype=bool))

    def step(carry, inp):
        C, n, m = carry
        qc, kc, vc, ic, fc = inp
        b = jnp.cumsum(jax.nn.log_sigmoid(fc), axis=-1)
        a = b + m[..., None]
        dmat = jnp.where(tri, b[..., :, None] - b[..., None, :] + ic[..., None, :], -jnp.inf)
        m_t = jnp.maximum(a, jnp.max(dmat, axis=-1))
        wa = jnp.exp(a - m_t)
        s = jnp.einsum('bhtd,bhsd->bhts', qc, kc) * jnp.exp(dmat - m_t[..., None])
        num = wa[..., None] * jnp.einsum('bhtd,bhde->bhte', qc, C) + jnp.einsum('bhts,bhse->bhte', s, vc)
        den = wa * jnp.einsum('bhtd,bhd->bht', qc, n) + jnp.sum(s, axis=-1)
        h = num / jnp.maximum(jnp.abs(den), jnp.exp(-m_t))[..., None]
        b_last = b[..., -1]
        g = b_last[..., None] - b + ic
        m_new = jnp.maximum(b_last + m, jnp.max(g, axis=-1))
        decay = jnp.exp(b_last + m - m_new)
        wg = jnp.exp(g - m_new[..., None])
        C_new = decay[..., None, None] * C + jnp.einsum('bhs,bhsd,bhse->bhde', wg, kc, vc)
        n_new = decay[..., None] * n + jnp.einsum('bhs,bhsd->bhd', wg, kc)
        return (C_new, n_new, m_new), h

    state, h = lax.scan(step, state, (chunks(q), chunks(k), chunks(v), chunks(i_pre), chunks(f_pre)))
    h = jnp.moveaxis(h, 0, 2).reshape(Bn, H, T, -1)
    return state, h.astype(v.dtype)


def diff_attention(q, k, v, lam):
    s = jnp.einsum('bqhmd,bkhmd->bhmqk', q, k).astype(jnp.float32) * (C_DH ** -0.5)
    p = jax.nn.softmax(s, axis=-1)
    w = p[:, :, 0] - lam * p[:, :, 1]
    return jnp.einsum('bhqk,bkhe->bqhe', w.astype(v.dtype), v)


def token_mixer(h_lat, h_ctx, rows, w_in, gmlp_norm_g, gmlp_ws, gmlp_bs, mlstm_gate_b, mlstm_norm_g,
                diff_lambda, diff_norm_g, w_branch, w_out, lambda_init, ctx_out):
    Bn, S, _ = h_lat.shape
    Tc = h_ctx.shape[1]
    zl = jnp.split(h_lat @ w_in, SPLIT_POINTS, axis=-1)
    zc = jnp.split(h_ctx @ w_in, SPLIT_POINTS, axis=-1)

    def mlstm_inputs(z, T):
        q = z[1].reshape(Bn, T, B_HEADS, B_DQK).transpose(0, 2, 1, 3) * (B_DQK ** -0.5)
        k = z[2].reshape(Bn, T, B_HEADS, B_DQK).transpose(0, 2, 1, 3)
        v = z[3].reshape(Bn, T, B_HEADS, B_DV).transpose(0, 2, 1, 3)
        g = (z[5].astype(jnp.float32).reshape(Bn, T, 2, 2, B_HEADS) + mlstm_gate_b).transpose(2, 3, 0, 4, 1)
        return q, k, v, g

    def flip(a):
        return jnp.flip(a, axis=2)

    qc, kc, vc, gc = mlstm_inputs(zc, Tc)
    ql, kl, vl, gl = mlstm_inputs(zl, S)
    zero = (jnp.zeros((Bn, B_HEADS, B_DQK, B_DV), jnp.float32),
            jnp.zeros((Bn, B_HEADS, B_DQK), jnp.float32),
            jnp.zeros((Bn, B_HEADS), jnp.float32))
    st_f, hc_f = mlstm_scan(qc, kc, vc, gc[0, 0], gc[0, 1], zero)
    _, hl_f = mlstm_scan(ql, kl, vl, gl[0, 0], gl[0, 1], st_f)
    st_b, hc_b = mlstm_scan(flip(qc), flip(kc), flip(vc), flip(gc[1, 0]), flip(gc[1, 1]), zero)
    _, hl_b = mlstm_scan(flip(ql), flip(kl), flip(vl), flip(gl[1, 0]), flip(gl[1, 1]), st_b)

    def mlstm_out(h, o, T):
        h = rmsnorm(h.transpose(0, 2, 1, 3), mlstm_norm_g.reshape(B_HEADS, B_DV))
        return jax.nn.sigmoid(o) * h.reshape(Bn, T, BRANCH_WIDTH)

    lam_p = diff_lambda.astype(jnp.float32)
    lam = jnp.exp(jnp.sum(lam_p[0] * lam_p[1])) - jnp.exp(jnp.sum(lam_p[2] * lam_p[3])) + lambda_init
    cos, sin = axial_rope(rows, C_DH)
    q_l = apply_rope(zl[6].reshape(Bn, S, C_HEADS, 2, C_DH), cos, sin)
    k_l = apply_rope(zl[7].reshape(Bn, S, C_HEADS, 2, C_DH), cos, sin)
    v_l = zl[8].reshape(Bn, S, C_HEADS, C_DV)
    k_c = zc[7].reshape(Bn, Tc, C_HEADS, 2, C_DH)
    v_c = zc[8].reshape(Bn, Tc, C_HEADS, C_DV)
    k_all = jnp.concatenate([k_c, k_l], axis=1)
    v_all = jnp.concatenate([v_c, v_l], axis=1)
    qb = jnp.moveaxis(q_l.reshape(Bn, S // CHUNK, CHUNK, C_HEADS, 2, C_DH), 1, 0)
    o_l = lax.map(lambda qi: diff_attention(qi, k_all, v_all, lam), qb)
    o_l = jnp.moveaxis(o_l, 0, 1).reshape(Bn, S, C_HEADS, C_DV)

    def diff_out(o, T):
        return (rmsnorm(o, diff_norm_g) * (1.0 - lambda_init)).reshape(Bn, T, BRANCH_WIDTH)

    def merge(ya, yb, yc, gate_pre):
        g = jax.nn.sigmoid(gate_pre.reshape(gate_pre.shape[:-1] + (N_BRANCH, D_MODEL)))
        y = (g[..., 0, :] * (ya @ w_branch[0]) + g[..., 1, :] * (yb @ w_branch[1])
             + g[..., 2, :] * (yc @ w_branch[2]))
        return y @ w_out

    y_lat = merge(gmlp_branch(zl[0], gmlp_norm_g, gmlp_ws, gmlp_bs),
                  mlstm_out(hl_f + flip(hl_b), zl[4], S),
                  diff_out(o_l, S), zl[9])
    if not ctx_out:
        return y_lat, None
    q_c = zc[6].reshape(Bn, Tc, C_HEADS, 2, C_DH)
    o_c = diff_attention(q_c, k_c, v_c, lam)
    y_ctx = merge(gmlp_branch(zc[0], gmlp_norm_g, gmlp_ws, gmlp_bs),
                  mlstm_out(hc_f + flip(hc_b), zc[4], Tc),
                  diff_out(o_c, Tc), zc[9])
    return y_lat, y_ctx


def setup_inputs(seed: int = 0) -> dict:
    key = jax.random.key(seed)
    ks = jax.random.split(key, 24)
    D = D_MODEL

    def nrm(k, shape, scale):
        return jax.random.normal(k, shape, jnp.float32) * scale

    i_bias = nrm(ks[14], (DEPTH, 2, 1, B_HEADS), 0.1)
    f_bias = jnp.linspace(3.0, 6.0, B_HEADS, dtype=jnp.float32) + nrm(ks[15], (DEPTH, 2, 1, B_HEADS), 0.1)
    return {
        'x': nrm(ks[0], (BATCH, SEQ, D), 1.0),
        'c': nrm(ks[1], (BATCH, D), 1.0),
        'ctx': nrm(ks[2], (BATCH, CTX_LEN, D), 1.0),
        'c_ctx': nrm(ks[3], (D,), 1.0),
        'ada_down': nrm(ks[4], (DEPTH, D, ADA_RANK), D ** -0.5),
        'ada_up': nrm(ks[5], (DEPTH, ADA_RANK, N_SUB * 3 * D), 0.3 * ADA_RANK ** -0.5),
        'ada_bias': nrm(ks[6], (DEPTH, N_SUB * 3 * D), 0.02),
        'norm_g': 1.0 + nrm(ks[7], (DEPTH, N_SUB, D), 0.02),
        'ffn_w13': nrm(ks[8], (DEPTH, 2, D, 2 * FFN_DIM), D ** -0.5),
        'ffn_w2': nrm(ks[9], (DEPTH, 2, FFN_DIM, D), FFN_DIM ** -0.5),
        'w_in': nrm(ks[10], (DEPTH, D, IN_COLS), D ** -0.5),
        'gmlp_norm_g': 1.0 + nrm(ks[11], (DEPTH, BRANCH_WIDTH), 0.02),
        'gmlp_ws': nrm(ks[12], (DEPTH, A_GROUPS, CHUNK, CHUNK), CHUNK ** -0.5),
        'gmlp_bs': 1.0 + nrm(ks[13], (DEPTH, A_GROUPS, CHUNK), 0.02),
        'mlstm_gate_b': jnp.concatenate([i_bias, f_bias], axis=2),
        'mlstm_norm_g': 1.0 + nrm(ks[16], (DEPTH, BRANCH_WIDTH), 0.02),
        'diff_lambda': nrm(ks[17], (DEPTH, 4, C_DH), 0.1),
        'diff_norm_g': 1.0 + nrm(ks[18], (DEPTH, C_DV), 0.02),
        'w_branch': nrm(ks[19], (DEPTH, N_BRANCH, BRANCH_WIDTH, D), BRANCH_WIDTH ** -0.5),
        'w_out': nrm(ks[20], (DEPTH, D, D), D ** -0.5),
        'final_g': 1.0 + nrm(ks[21], (D,), 0.02),
    }


def reference(x, c, ctx, c_ctx, ada_down, ada_up, ada_bias, norm_g, ffn_w13, ffn_w2, w_in,
              gmlp_norm_g, gmlp_ws, gmlp_bs, mlstm_gate_b, mlstm_norm_g, diff_lambda, diff_norm_g,
              w_branch, w_out, final_g):
    rows = x.shape[1] // GRID_W
    for l in range(DEPTH):
        last = l == DEPTH - 1
        lambda_init = 0.8 - 0.6 * math.exp(-0.3 * l)
        ml = modulation(c, ada_down[l], ada_up[l], ada_bias[l])[:, None]
        mc = modulation(c_ctx, ada_down[l], ada_up[l], ada_bias[l])[None, None]
        x = ffn_sublayer(x, norm_g[l, 0], ml[:, :, 0], ffn_w13[l, 0], ffn_w2[l, 0])
        ctx = ffn_sublayer(ctx, norm_g[l, 0], mc[:, :, 0], ffn_w13[l, 0], ffn_w2[l, 0])
        h_lat = modulate(x, norm_g[l, 1], ml[:, :, 1, 0], ml[:, :, 1, 1])
        h_ctx = modulate(ctx, norm_g[l, 1], mc[:, :, 1, 0], mc[:, :, 1, 1])
        y_lat, y_ctx = token_mixer(h_lat, h_ctx, rows, w_in[l], gmlp_norm_g[l], gmlp_ws[l], gmlp_bs[l],
                                   mlstm_gate_b[l], mlstm_norm_g[l], diff_lambda[l], diff_norm_g[l],
                                   w_branch[l], w_out[l], lambda_init, not last)
        x = x + ml[:, :, 1, 2] * y_lat
        x = ffn_sublayer(x, norm_g[l, 2], ml[:, :, 2], ffn_w13[l, 1], ffn_w2[l, 1])
        if not last:
            ctx = ctx + mc[:, :, 1, 2] * y_ctx
            ctx = ffn_sublayer(ctx, norm_g[l, 2], mc[:, :, 2], ffn_w13[l, 1], ffn_w2[l, 1])
    return rmsnorm(x, final_g)
```

```python
import functools
import math

import jax
import jax.numpy as jnp
from jax import lax
from jax.experimental import pallas as pl
from jax.experimental.pallas import tpu as pltpu

F32 = jnp.float32
BF16 = jnp.bfloat16

GRID_W = 64
CHUNK = 128
BRANCH_WIDTH = 1024
A_GROUPS = 8
A_GDIM = BRANCH_WIDTH // A_GROUPS
B_HEADS = 4
B_DQK = 128
B_DV = BRANCH_WIDTH // B_HEADS
C_HEADS = 4
C_DH = 128
C_DV = 2 * C_DH
N_BRANCH = 3
N_SUB = 3
ROPE_THETA = 10000.0
EPS = 1e-6
GATE_COLS = 4 * B_HEADS
LANES = 128
Z_UV, Z_MQ, Z_MK, Z_MV, Z_MO, Z_AQ, Z_AK, Z_AV, Z_GATE = (
    0, 2048, 2560, 3072, 4096, 5120, 6144, 7168, 8192)
ROW_TILE = 512
VMEM_LIMIT = 56 * 1024 * 1024


def _params(sem):
    return pltpu.CompilerParams(dimension_semantics=sem, vmem_limit_bytes=VMEM_LIMIT)


def _dot(a, b):
    return jnp.dot(a, b, preferred_element_type=F32)


def _dot_nt(a, b):
    return lax.dot_general(a, b, (((1,), (1,)), ((), ())), preferred_element_type=F32)


def _sigmoid(x):
    return 1.0 / (1.0 + jnp.exp(-x))


def _log_sigmoid(x):
    return jnp.minimum(x, 0.0) - jnp.log(1.0 + jnp.exp(-jnp.abs(x)))


def _norm_mod(x, g, shift, scale):
    y = x * lax.rsqrt(jnp.mean(x * x, axis=-1, keepdims=True) + EPS)
    return (y * g) * (1.0 + scale) + shift


def _mod_kernel(cv_ref, down_ref, up_ref, b_ref, o_ref):
    s = cv_ref[...]
    s = s * _sigmoid(s)
    t = _dot(s.astype(BF16), down_ref[...].astype(BF16))
    o_ref[...] = _dot(t.astype(BF16), up_ref[...].astype(BF16)) + b_ref[...]


def _modulation(cv, ada_down, ada_up, ada_bias):
    depth, d, r = ada_down.shape
    n = ada_up.shape[-1]
    tn = d
    return pl.pallas_call(
        _mod_kernel,
        grid=(depth, n // tn),
        in_specs=[
            pl.BlockSpec((8, d), lambda l, j: (0, 0)),
            pl.BlockSpec((None, d, r), lambda l, j: (l, 0, 0)),
            pl.BlockSpec((None, r, tn), lambda l, j: (l, 0, j)),
            pl.BlockSpec((None, 1, tn), lambda l, j: (l, 0, j)),
        ],
        out_specs=pl.BlockSpec((None, 8, tn), lambda l, j: (l, 0, j)),
        out_shape=jax.ShapeDtypeStruct((depth, 8, n), F32),
        compiler_params=_params(("parallel", "arbitrary")),
        name="adaln_modulation",
    )(cv, ada_down, ada_up, ada_bias.reshape(depth, 1, n))


def _ffn_kernel(x_ref, g_ref, mod_ref, wa_ref, wb_ref, w2_ref, o_ref, h_ref, *, nj):
    j = pl.program_id(1)

    @pl.when(j == 0)
    def _():
        h = _norm_mod(x_ref[...], g_ref[...], mod_ref[0:1, :], mod_ref[1:2, :])
        h_ref[...] = h.astype(BF16)
        o_ref[...] = jnp.zeros_like(o_ref)

    h = h_ref[...]
    a = _dot(h, wa_ref[...])
    b = _dot(h, wb_ref[...])
    act = (a * _sigmoid(a) * b).astype(BF16)
    o_ref[...] += _dot(act, w2_ref[...])

    @pl.when(j == nj - 1)
    def _():
        o_ref[...] = x_ref[...] + 0.5 * mod_ref[2:3, :] * o_ref[...]


def _ffn(xs, g, mod, w13, w2, l, s, n_lat_tiles):
    mp, d = xs.shape
    f = w2.shape[2]
    tm = ROW_TILE
    tf = min(f, 256)
    nj = f // tf
    return pl.pallas_call(
        functools.partial(_ffn_kernel, nj=nj),
        grid=(mp // tm, nj),
        in_specs=[
            pl.BlockSpec((tm, d), lambda i, j: (i, 0), pipeline_mode=pl.Buffered(1)),
            pl.BlockSpec((1, d), lambda i, j: (0, 0)),
            pl.BlockSpec((None, 3, d), lambda i, j: (jnp.minimum(i // n_lat_tiles, 1), 0, 0)),
            pl.BlockSpec((None, None, d, tf), lambda i, j: (l, s, 0, j)),
            pl.BlockSpec((None, None, d, tf), lambda i, j: (l, s, 0, nj + j)),
            pl.BlockSpec((None, None, tf, d), lambda i, j: (l, s, j, 0)),
        ],
        out_specs=pl.BlockSpec((tm, d), lambda i, j: (i, 0)),
        out_shape=jax.ShapeDtypeStruct((mp, d), F32),
        scratch_shapes=[pltpu.VMEM((tm, d), BF16)],
        compiler_params=_params(("parallel", "arbitrary")),
        name="ffn_half_step",
    )(xs, g, mod, w13, w13, w2)


def _proj_kernel(x_ref, g_ref, mod_ref, w_ref, wg_ref, gb_ref, z_ref, zg_ref, h_ref):
    j = pl.program_id(1)

    @pl.when(j == 0)
    def _():
        h = _norm_mod(x_ref[...], g_ref[...], mod_ref[0:1, :], mod_ref[1:2, :]).astype(BF16)
        h_ref[...] = h
        zg_ref[...] = _dot(h, wg_ref[...]) + gb_ref[...]

    z_ref[...] = _dot(h_ref[...], w_ref[...]).astype(z_ref.dtype)


def _proj(xs, g, mod, w_main, w_gate, gate_b, l, n_lat_tiles):
    mp, d = xs.shape
    n = w_main.shape[-1]
    tm = ROW_TILE
    tn = 1024
    return pl.pallas_call(
        _proj_kernel,
        grid=(mp // tm, n // tn),
        in_specs=[
            pl.BlockSpec((tm, d), lambda i, j: (i, 0)),
            pl.BlockSpec((1, d), lambda i, j: (0, 0)),
            pl.BlockSpec((None, 3, d), lambda i, j: (jnp.minimum(i // n_lat_tiles, 1), 0, 0)),
            pl.BlockSpec((None, d, tn), lambda i, j: (l, 0, j)),
            pl.BlockSpec((None, d, LANES), lambda i, j: (l, 0, 0)),
            pl.BlockSpec((1, LANES), lambda i, j: (0, 0)),
        ],
        out_specs=[
            pl.BlockSpec((tm, tn), lambda i, j: (i, j)),
            pl.BlockSpec((tm, LANES), lambda i, j: (i, 0)),
        ],
        out_shape=[
            jax.ShapeDtypeStruct((mp, n), BF16),
            jax.ShapeDtypeStruct((mp, LANES), F32),
        ],
        scratch_shapes=[pltpu.VMEM((tm, d), BF16)],
        compiler_params=_params(("parallel", "arbitrary")),
        name="mixer_in_proj",
    )(xs, g, mod, w_main, w_gate, gate_b)


def _gmlp_kernel(u_ref, v_ref, ng_ref, ws_ref, bst_ref, o_ref, *, tm):
    v = jax.nn.gelu(v_ref[...].astype(F32))
    vc = v - jnp.mean(v, axis=-1, keepdims=True)
    vn = vc * lax.rsqrt(jnp.mean(vc * vc, axis=-1, keepdims=True) + EPS) * ng_ref[...]
    vn = vn.astype(BF16)
    for c in range(tm // CHUNK):
        rows = slice(c * CHUNK, (c + 1) * CHUNK)
        for g in range(A_GROUPS):
            cols = slice(g * A_GDIM, (g + 1) * A_GDIM)
            mixed = _dot(ws_ref[g].astype(BF16), vn[rows, cols]) + bst_ref[:, g:g + 1]
            u = jax.nn.gelu(u_ref[rows, cols].astype(F32))
            o_ref[rows, cols] = (u * mixed).astype(o_ref.dtype)


def _gmlp(z, ng, ws, bst):
    mp = z.shape[0]
    tm = ROW_TILE
    bw = BRANCH_WIDTH
    return pl.pallas_call(
        functools.partial(_gmlp_kernel, tm=tm),
        grid=(mp // tm,),
        in_specs=[
            pl.BlockSpec((tm, bw), lambda i: (i, Z_UV // bw)),
            pl.BlockSpec((tm, bw), lambda i: (i, Z_UV // bw + 1)),
            pl.BlockSpec((1, bw), lambda i: (0, 0)),
            pl.BlockSpec((A_GROUPS, CHUNK, CHUNK), lambda i: (0, 0, 0)),
            pl.BlockSpec((CHUNK, A_GROUPS), lambda i: (0, 0)),
        ],
        out_specs=pl.BlockSpec((tm, bw), lambda i: (i, 0)),
        out_shape=jax.ShapeDtypeStruct((mp, bw), BF16),
        compiler_params=_params(("parallel",)),
        name="gmlp_branch",
    )(z, z, ng, ws, bst)


def _mlstm_head(rev, q, k, v, gi_col, gf_col, gi_row, gf_row, c_state, n_state, m_state):
    L = q.shape[0]
    r = lax.broadcasted_iota(jnp.int32, (L, L), 0)
    c = lax.broadcasted_iota(jnp.int32, (L, L), 1)
    seen = (c >= r) if rev else (c <= r)
    seen_t = (r >= c) if rev else (r <= c)
    lf_col = _log_sigmoid(gf_col)
    lf_row = _log_sigmoid(gf_row)
    b_col = jnp.sum(jnp.where(seen, lf_row, 0.0), axis=1, keepdims=True)
    b_row = jnp.sum(jnp.where(seen_t, lf_col, 0.0), axis=0, keepdims=True)
    b_last = jnp.sum(lf_row, axis=1, keepdims=True)

    dmat = jnp.where(seen, b_col - b_row + gi_row, -jnp.inf)
    a_col = b_col + m_state
    m_t = jnp.maximum(a_col, jnp.max(dmat, axis=1, keepdims=True))
    wa = jnp.exp(a_col - m_t)
    s = _dot_nt(q, k) * jnp.exp(dmat - m_t)
    num = wa * _dot(q, c_state.astype(BF16)) + _dot(s.astype(BF16), v)
    qn = jnp.sum(q.astype(F32) * n_state, axis=1, keepdims=True)
    den = wa * qn + jnp.sum(s, axis=1, keepdims=True)
    h = num / jnp.maximum(jnp.abs(den), jnp.exp(-m_t))

    g_row = b_last - b_row + gi_row
    g_col = b_last - b_col + gi_col
    m_new = jnp.maximum(b_last + m_state, jnp.max(g_row, axis=1, keepdims=True))
    decay = jnp.exp(b_last + m_state - m_new)
    kw = k.astype(F32) * jnp.exp(g_col - m_new)
    c_new = decay * c_state + _dot(kw.T.astype(BF16), v)
    n_new = decay * n_state + jnp.sum(kw, axis=0, keepdims=True)
    return h, c_new, n_new, m_new


def _mlstm_kernel(qf_ref, kf_ref, vf_ref, gf_ref, qb_ref, kb_ref, vb_ref, gb_ref,
                  hf_ref, hb_ref, c_ref, n_ref, m_ref, *, n_steps):
    t = pl.program_id(0)

    @pl.when(t == 0)
    def _():
        c_ref[...] = jnp.zeros_like(c_ref)
        n_ref[...] = jnp.zeros_like(n_ref)
        m_ref[...] = jnp.zeros_like(m_ref)

    @pl.when(t < n_steps)
    def _():
        for d, (q_ref, k_ref, v_ref, g_ref, h_ref) in enumerate(
                ((qf_ref, kf_ref, vf_ref, gf_ref, hf_ref), (qb_ref, kb_ref, vb_ref, gb_ref, hb_ref))):
            gates = g_ref[...]
            gates_t = gates.T
            for hd in range(B_HEADS):
                idx = d * B_HEADS + hd
                ci = d * 2 * B_HEADS + hd
                cf = ci + B_HEADS
                qs = slice(hd * B_DQK, (hd + 1) * B_DQK)
                vs = slice(hd * B_DV, (hd + 1) * B_DV)
                q = (q_ref[:, qs].astype(F32) * (B_DQK ** -0.5)).astype(BF16)
                h, c_new, n_new, m_new = _mlstm_head(
                    d == 1, q, k_ref[:, qs], v_ref[:, vs],
                    gates[:, ci:ci + 1], gates[:, cf:cf + 1],
                    gates_t[ci:ci + 1, :], gates_t[cf:cf + 1, :],
                    c_ref[idx], n_ref[idx], m_ref[idx][:, 0:1])
                h_ref[:, vs] = h
                c_ref[idx] = c_new
                n_ref[idx] = n_new
                m_ref[idx] = jnp.broadcast_to(m_new, (1, LANES))

    @pl.when(t >= n_steps)
    def _():
        hf_ref[...] = jnp.zeros_like(hf_ref)
        hb_ref[...] = jnp.zeros_like(hb_ref)


def _mlstm(z, zg, n_lat, n_ctx):
    mp = z.shape[0]
    n_steps = n_lat + n_ctx
    n_blk = mp // CHUNK

    def fwd(t):
        return jnp.where(t < n_steps, (t + n_lat) % n_steps, t)

    def bwd(t):
        return jnp.where(t < n_steps, n_steps - 1 - t, t)

    def specs(order):
        return [
            pl.BlockSpec((CHUNK, B_HEADS * B_DQK), lambda t: (order(t), Z_MQ // (B_HEADS * B_DQK))),
            pl.BlockSpec((CHUNK, B_HEADS * B_DQK), lambda t: (order(t), Z_MK // (B_HEADS * B_DQK))),
            pl.BlockSpec((CHUNK, BRANCH_WIDTH), lambda t: (order(t), Z_MV // BRANCH_WIDTH)),
            pl.BlockSpec((CHUNK, LANES), lambda t: (order(t), 0)),
        ]

    return pl.pallas_call(
        functools.partial(_mlstm_kernel, n_steps=n_steps),
        grid=(n_blk,),
        in_specs=specs(fwd) + specs(bwd),
        out_specs=[
            pl.BlockSpec((CHUNK, BRANCH_WIDTH), lambda t: (fwd(t), 0)),
            pl.BlockSpec((CHUNK, BRANCH_WIDTH), lambda t: (bwd(t), 0)),
        ],
        out_shape=[jax.ShapeDtypeStruct((mp, BRANCH_WIDTH), F32)] * 2,
        scratch_shapes=[
            pltpu.VMEM((2 * B_HEADS, B_DQK, B_DV), F32),
            pltpu.VMEM((2 * B_HEADS, 1, B_DQK), F32),
            pltpu.VMEM((2 * B_HEADS, 1, LANES), F32),
        ],
        compiler_params=_params(("arbitrary",)),
        name="mlstm_bidir",
    )(z, z, z, zg, z, z, z, zg)


def _rope_kernel(q_ref, k_ref, cos_ref, sin_ref, qo_ref, ko_ref):
    cos = cos_ref[...]
    sin = sin_ref[...]
    for src, dst in ((q_ref, qo_ref), (k_ref, ko_ref)):
        for g in range(2 * C_HEADS):
            cols = slice(g * C_DH, (g + 1) * C_DH)
            x = src[:, cols].astype(F32)
            dst[:, cols] = (x * cos + pltpu.roll(x, C_DH // 2, 1) * sin).astype(dst.dtype)


def _rope(z, cos_t, sin_t):
    mp = z.shape[0]
    tm = ROW_TILE
    w = 2 * C_HEADS * C_DH
    return pl.pallas_call(
        _rope_kernel,
        grid=(mp // tm,),
        in_specs=[
            pl.BlockSpec((tm, w), lambda i: (i, Z_AQ // w)),
            pl.BlockSpec((tm, w), lambda i: (i, Z_AK // w)),
            pl.BlockSpec((tm, C_DH), lambda i: (i, 0)),
            pl.BlockSpec((tm, C_DH), lambda i: (i, 0)),
        ],
        out_specs=[pl.BlockSpec((tm, w), lambda i: (i, 0))] * 2,
        out_shape=[jax.ShapeDtypeStruct((mp, w), BF16)] * 2,
        compiler_params=_params(("parallel",)),
        name="axial_rope",
    )(z, z, cos_t, sin_t)


def _attn_kernel(q_ref, k_ref, v_ref, lam_ref, ng_ref, o_ref,
                 m1_ref, l1_ref, a1_ref, m2_ref, l2_ref, a2_ref, *, nk, lam_init):
    kk = pl.program_id(2)
    scale = C_DH ** -0.5

    @pl.when(kk == 0)
    def _():
        for m_ref, l_ref, a_ref in ((m1_ref, l1_ref, a1_ref), (m2_ref, l2_ref, a2_ref)):
            m_ref[...] = jnp.full_like(m_ref, -jnp.inf)
            l_ref[...] = jnp.zeros_like(l_ref)
            a_ref[...] = jnp.zeros_like(a_ref)

    v = v_ref[...]
    for mp_i, (m_ref, l_ref, a_ref) in enumerate(((m1_ref, l1_ref, a1_ref), (m2_ref, l2_ref, a2_ref))):
        cols = slice(mp_i * C_DH, (mp_i + 1) * C_DH)
        s = _dot_nt(q_ref[:, cols], k_ref[:, cols])
        m_prev = m_ref[...]
        m_new = jnp.maximum(m_prev, jnp.max(s, axis=-1, keepdims=True))
        alpha = jnp.exp((m_prev - m_new) * scale)
        p = jnp.exp((s - m_new) * scale)
        l_ref[...] = alpha * l_ref[...] + jnp.sum(p, axis=-1, keepdims=True)
        a_ref[...] = alpha * a_ref[...] + _dot(p.astype(BF16), v)
        m_ref[...] = m_new

    @pl.when(kk == nk - 1)
    def _():
        lp = lam_ref[...]
        lam = (jnp.exp(jnp.sum(lp[0:1] * lp[1:2], axis=-1, keepdims=True))
               - jnp.exp(jnp.sum(lp[2:3] * lp[3:4], axis=-1, keepdims=True)) + lam_init)
        o = a1_ref[...] / l1_ref[...] - lam * (a2_ref[...] / l2_ref[...])
        y = o * lax.rsqrt(jnp.mean(o * o, axis=-1, keepdims=True) + EPS) * ng_ref[...]
        o_ref[...] = (y * (1.0 - lam_init)).astype(o_ref.dtype)


def _attention(qr, kr, z, lam_p, ng, lam_init, q_row0, n_q_rows, kv_row0, n_kv_rows, tq, tk):
    nq = n_q_rows // tq
    nk = n_kv_rows // tk
    qb0 = q_row0 // tq
    kb0 = kv_row0 // tk
    return pl.pallas_call(
        functools.partial(_attn_kernel, nk=nk, lam_init=lam_init),
        grid=(C_HEADS, nq, nk),
        in_specs=[
            pl.BlockSpec((tq, C_DV), lambda h, i, k: (qb0 + i, h)),
            pl.BlockSpec((tk, C_DV), lambda h, i, k: (kb0 + k, h)),
            pl.BlockSpec((tk, C_DV), lambda h, i, k: (kb0 + k, Z_AV // C_DV + h)),
            pl.BlockSpec((4, C_DH), lambda h, i, k: (0, 0)),
            pl.BlockSpec((1, C_DV), lambda h, i, k: (0, 0)),
        ],
        out_specs=pl.BlockSpec((tq, C_DV), lambda h, i, k: (i, h)),
        out_shape=jax.ShapeDtypeStruct((n_q_rows, C_HEADS * C_DV), BF16),
        scratch_shapes=[
            pltpu.VMEM((tq, 1), F32), pltpu.VMEM((tq, 1), F32), pltpu.VMEM((tq, C_DV), F32),
            pltpu.VMEM((tq, 1), F32), pltpu.VMEM((tq, 1), F32), pltpu.VMEM((tq, C_DV), F32),
        ],
        compiler_params=_params(("parallel", "parallel", "arbitrary")),
        name="diff_attention",
    )(qr, kr, z, lam_p, ng)


def _merge_kernel(ya_ref, hf_ref, hb_ref, og_ref, yc_ref, mng_ref, g0_ref, g1_ref, g2_ref, wb_ref,
                  y_ref, yb_ref):
    j = pl.program_id(1)

    @pl.when(j == 0)
    def _():
        for hd in range(B_HEADS):
            cols = slice(hd * B_DV, (hd + 1) * B_DV)
            h = hf_ref[:, cols] + hb_ref[:, cols]
            hn = h * lax.rsqrt(jnp.mean(h * h, axis=-1, keepdims=True) + EPS) * mng_ref[:, cols]
            yb_ref[:, cols] = (_sigmoid(og_ref[:, cols].astype(F32)) * hn).astype(BF16)

    y = _sigmoid(g0_ref[...].astype(F32)) * _dot(ya_ref[...], wb_ref[0])
    y += _sigmoid(g1_ref[...].astype(F32)) * _dot(yb_ref[...], wb_ref[1])
    y += _sigmoid(g2_ref[...].astype(F32)) * _dot(yc_ref[...], wb_ref[2])
    y_ref[...] = y.astype(y_ref.dtype)


def _merge(ya, hf, hb, z, yc, mng, w_branch, l):
    mp = z.shape[0]
    d = w_branch.shape[-1]
    tm = ROW_TILE
    tn = min(d, 1024)
    bw = BRANCH_WIDTH
    gate_blk = Z_GATE // tn
    row = lambda i, j: (i, 0)
    return pl.pallas_call(
        _merge_kernel,
        grid=(mp // tm, d // tn),
        in_specs=[
            pl.BlockSpec((tm, bw), row),
            pl.BlockSpec((tm, bw), row),
            pl.BlockSpec((tm, bw), row),
            pl.BlockSpec((tm, bw), lambda i, j: (i, Z_MO // bw)),
            pl.BlockSpec((tm, bw), row),
            pl.BlockSpec((1, bw), lambda i, j: (0, 0)),
            pl.BlockSpec((tm, tn), lambda i, j: (i, gate_blk + j)),
            pl.BlockSpec((tm, tn), lambda i, j: (i, gate_blk + d // tn + j)),
            pl.BlockSpec((tm, tn), lambda i, j: (i, gate_blk + 2 * (d // tn) + j)),
            pl.BlockSpec((None, N_BRANCH, bw, tn), lambda i, j: (l, 0, 0, j)),
        ],
        out_specs=pl.BlockSpec((tm, tn), lambda i, j: (i, j)),
        out_shape=jax.ShapeDtypeStruct((mp, d), BF16),
        scratch_shapes=[pltpu.VMEM((tm, bw), BF16)],
        compiler_params=_params(("parallel", "arbitrary")),
        name="branch_merge",
    )(ya, hf, hb, z, yc, mng, z, z, z, w_branch)


def _out_kernel(x_ref, y_ref, w_ref, gate_ref, o_ref):
    o_ref[...] = x_ref[...] + gate_ref[...] * _dot(y_ref[...], w_ref[...])


def _out_proj(xs, y, w_out, mod_gate, l, n_lat_tiles):
    mp, d = xs.shape
    tm = ROW_TILE
    tn = min(d, 1024)
    return pl.pallas_call(
        _out_kernel,
        grid=(mp // tm, d // tn),
        in_specs=[
            pl.BlockSpec((tm, tn), lambda i, j: (i, j)),
            pl.BlockSpec((tm, d), lambda i, j: (i, 0)),
            pl.BlockSpec((None, d, tn), lambda i, j: (l, 0, j)),
            pl.BlockSpec((None, 1, tn), lambda i, j: (jnp.minimum(i // n_lat_tiles, 1), 0, j)),
        ],
        out_specs=pl.BlockSpec((tm, tn), lambda i, j: (i, j)),
        out_shape=jax.ShapeDtypeStruct((mp, d), F32),
        compiler_params=_params(("parallel", "arbitrary")),
        name="mixer_out_proj",
    )(xs, y, w_out, mod_gate)


def _final_kernel(x_ref, g_ref, o_ref):
    x = x_ref[...]
    o_ref[...] = x * lax.rsqrt(jnp.mean(x * x, axis=-1, keepdims=True) + EPS) * g_ref[...]


def _final_norm(xs, g, n_rows):
    d = xs.shape[1]
    tm = ROW_TILE
    return pl.pallas_call(
        _final_kernel,
        grid=(n_rows // tm,),
        in_specs=[pl.BlockSpec((tm, d), lambda i: (i, 0)), pl.BlockSpec((1, d), lambda i: (0, 0))],
        out_specs=pl.BlockSpec((tm, d), lambda i: (i, 0)),
        out_shape=jax.ShapeDtypeStruct((n_rows, d), F32),
        compiler_params=_params(("parallel",)),
        name="final_norm",
    )(xs, g)


def _pack_w_in(w_in):
    g0 = Z_MO + BRANCH_WIDTH
    q0 = g0 + GATE_COLS
    k0 = q0 + 2 * C_HEADS * C_DH
    v0 = k0 + 2 * C_HEADS * C_DH

    def deinterleave(w):
        lead = w.shape[:-1]
        w = w.reshape(lead + (2 * C_HEADS, C_DH // 2, 2))
        return jnp.swapaxes(w, -1, -2).reshape(lead + (2 * C_HEADS * C_DH,))

    w_main = jnp.concatenate(
        [w_in[..., :g0].astype(BF16), deinterleave(w_in[..., q0:k0]).astype(BF16),
         deinterleave(w_in[..., k0:v0]).astype(BF16), w_in[..., v0:].astype(BF16)], axis=-1)
    w_gate = jnp.pad(w_in[..., g0:q0], ((0, 0), (0, 0), (0, LANES - GATE_COLS))).astype(BF16)
    return w_main, w_gate


def _rope_tables(n_lat, n_rows):
    t = jnp.arange(n_lat)
    r = (t // GRID_W).astype(F32)
    col = (t % GRID_W).astype(F32)
    n_freq = C_DH // 4
    inv = ROPE_THETA ** (-jnp.arange(n_freq, dtype=F32) / n_freq)
    ang = jnp.concatenate([r[:, None] * inv, col[:, None] * inv], axis=-1)
    cos = jnp.concatenate([jnp.cos(ang), jnp.cos(ang)], axis=-1)
    sin = jnp.concatenate([-jnp.sin(ang), jnp.sin(ang)], axis=-1)
    pad = n_rows - n_lat
    cos = jnp.concatenate([cos, jnp.ones((pad, C_DH), F32)], axis=0)
    sin = jnp.concatenate([sin, jnp.zeros((pad, C_DH), F32)], axis=0)
    return cos, sin


def kernel(x, c, ctx, c_ctx, ada_down, ada_up, ada_bias, norm_g, ffn_w13, ffn_w2, w_in, gmlp_norm_g,
           gmlp_ws, gmlp_bs, mlstm_gate_b, mlstm_norm_g, diff_lambda, diff_norm_g, w_branch, w_out,
           final_g):
    bsz, seq, d = x.shape
    n_ctx = ctx.shape[1]
    depth = ada_down.shape[0]
    assert bsz == 1 and seq % ROW_TILE == 0 and n_ctx % CHUNK == 0
    assert w_in.shape[-1] == Z_GATE + GATE_COLS + N_BRANCH * d
    n_tok = seq + n_ctx
    mp = -(-n_tok // ROW_TILE) * ROW_TILE
    n_lat_tiles = seq // ROW_TILE

    xs = jnp.concatenate([x[0], ctx[0], jnp.zeros((mp - n_tok, d), F32)], axis=0)
    w13 = ffn_w13.astype(BF16)
    w2 = ffn_w2.astype(BF16)
    w_main, w_gate = _pack_w_in(w_in)
    wbr = w_branch.astype(BF16)
    wo = w_out.astype(BF16)
    cos_t, sin_t = _rope_tables(seq, mp)
    gate_b = jnp.pad(mlstm_gate_b.reshape(depth, 1, GATE_COLS), ((0, 0), (0, 0), (0, LANES - GATE_COLS)))
    cv = jnp.concatenate([c, c_ctx[None], jnp.zeros((6, d), F32)], axis=0)

    mod = _modulation(cv, ada_down, ada_up, ada_bias)[:, :2].reshape(depth, 2, N_SUB, 3, d)

    tk = 768 if n_tok % 768 == 0 else 256
    for l in range(depth):
        lam_init = 0.8 - 0.6 * math.exp(-0.3 * l)
        xs = _ffn(xs, norm_g[l, 0:1], mod[l, :, 0], w13, w2, l, 0, n_lat_tiles)

        z, zg = _proj(xs, norm_g[l, 1:2], mod[l, :, 1], w_main, w_gate, gate_b[l], l, n_lat_tiles)
        ya = _gmlp(z, gmlp_norm_g[l][None], gmlp_ws[l], gmlp_bs[l].T)
        hf, hb = _mlstm(z, zg, seq // CHUNK, n_ctx // CHUNK)
        qr, kr = _rope(z, cos_t, sin_t)
        ng = diff_norm_g[l][None]
        yc_lat = _attention(qr, kr, z, diff_lambda[l], ng, lam_init, 0, seq, 0, n_tok, ROW_TILE, tk)
        yc_ctx = _attention(qr, kr, z, diff_lambda[l], ng, lam_init, seq, n_ctx, seq, n_ctx, n_ctx, n_ctx)
        yc = jnp.concatenate([yc_lat, yc_ctx, jnp.zeros((mp - n_tok, BRANCH_WIDTH), BF16)], axis=0)
        y = _merge(ya, hf, hb, z, yc, mlstm_norm_g[l][None], wbr, l)
        xs = _out_proj(xs, y, wo, mod[l, :, 1, 2:3], l, n_lat_tiles)

        xs = _ffn(xs, norm_g[l, 2:3], mod[l, :, 2], w13, w2, l, 1, n_lat_tiles)
    return _final_norm(xs, final_g[None], seq)[None]
```

```python
import functools
import math

import jax
import jax.numpy as jnp
from jax import lax
from jax.experimental import pallas as pl
from jax.experimental.pallas import tpu as pltpu

F32 = jnp.float32
BF16 = jnp.bfloat16

GRID_W = 64
CHUNK = 128
BRANCH_WIDTH = 1024
A_GROUPS = 8
A_GDIM = BRANCH_WIDTH // A_GROUPS
B_HEADS = 4
B_DQK = 128
B_DV = BRANCH_WIDTH // B_HEADS
C_HEADS = 4
C_DH = 128
C_DV = 2 * C_DH
N_BRANCH = 3
N_SUB = 3
ROPE_THETA = 10000.0
EPS = 1e-6
GATE_COLS = 4 * B_HEADS
LANES = 128
Z_UV, Z_MQ, Z_MK, Z_MV, Z_MO, Z_AQ, Z_AK, Z_AV, Z_GATE = (
    0, 2048, 2560, 3072, 4096, 5120, 6144, 7168, 8192)
DENSE_ROWS = 2816
ROW_SPLIT = 4
ELEM_ROWS = 256
ATTN_ROW_SPLIT = 4
VMEM_LIMIT = 56 * 1024 * 1024
LOG2E = math.log2(math.e)


def _params(sem):
    return pltpu.CompilerParams(dimension_semantics=sem, vmem_limit_bytes=VMEM_LIMIT)


def _dot(a, b):
    return jnp.dot(a, b, preferred_element_type=F32)


def _dot_nt(a, b):
    return lax.dot_general(a, b, (((1,), (1,)), ((), ())), preferred_element_type=F32)


def _sigmoid(x):
    return 1.0 / (1.0 + jnp.exp(-x))


def _log_sigmoid(x):
    return jnp.minimum(x, 0.0) - jnp.log(1.0 + jnp.exp(-jnp.abs(x)))


def _row_tile(n, cap):
    unit = 16 * ROW_SPLIT
    return max(t for t in range(unit, min(n, cap) + 1, unit) if n % t == 0)


def _single(block, index_map):
    return pl.BlockSpec(block, index_map, pipeline_mode=pl.Buffered(1))


def _for_row_subtiles(tm, body):
    sub = tm // ROW_SPLIT

    def step(r, carry):
        body(pl.ds(pl.multiple_of(r * sub, sub), sub), r * sub)
        return carry

    lax.fori_loop(0, ROW_SPLIT, step, 0)


def _is_ctx(row0, n, seq):
    return row0 + lax.broadcasted_iota(jnp.int32, (n, 1), 0) >= seq


def _mod_kernel(cv_ref, down_ref, up_ref, b_ref, o_ref):
    s = cv_ref[...]
    s = s * _sigmoid(s)
    t = _dot(s.astype(BF16), down_ref[...].astype(BF16))
    o_ref[...] = _dot(t.astype(BF16), up_ref[...].astype(BF16)) + b_ref[...]


def _modulation(cv, ada_down, ada_up, ada_bias):
    depth, d, r = ada_down.shape
    n = ada_up.shape[-1]
    tn = d
    return pl.pallas_call(
        _mod_kernel,
        grid=(depth, n // tn),
        in_specs=[
            pl.BlockSpec((8, d), lambda l, j: (0, 0)),
            pl.BlockSpec((None, d, r), lambda l, j: (l, 0, 0)),
            pl.BlockSpec((None, r, tn), lambda l, j: (l, 0, j)),
            pl.BlockSpec((None, 1, tn), lambda l, j: (l, 0, j)),
        ],
        out_specs=pl.BlockSpec((None, 8, tn), lambda l, j: (l, 0, j)),
        out_shape=jax.ShapeDtypeStruct((depth, 8, n), F32),
        compiler_params=_params(("parallel", "arbitrary")),
        name="adaln_modulation",
    )(cv, ada_down, ada_up, ada_bias.reshape(depth, 1, n))


def _normmod_kernel(x_ref, g_ref, mod_ref, h_ref, *, seq, tm):
    x = x_ref[...]
    ctx = _is_ctx(pl.program_id(0) * tm, tm, seq)
    shift = jnp.where(ctx, mod_ref[1, 0:1, :], mod_ref[0, 0:1, :])
    scale = jnp.where(ctx, mod_ref[1, 1:2, :], mod_ref[0, 1:2, :])
    y = x * lax.rsqrt(jnp.mean(x * x, axis=-1, keepdims=True) + EPS)
    h_ref[...] = ((y * g_ref[...]) * (1.0 + scale) + shift).astype(h_ref.dtype)


def _normmod(xs, g, mod, seq):
    n, d = xs.shape
    tm = ELEM_ROWS
    return pl.pallas_call(
        functools.partial(_normmod_kernel, seq=seq, tm=tm),
        grid=(n // tm,),
        in_specs=[
            pl.BlockSpec((tm, d), lambda i: (i, 0)),
            pl.BlockSpec((1, d), lambda i: (0, 0)),
            pl.BlockSpec((2, 3, d), lambda i: (0, 0, 0)),
        ],
        out_specs=pl.BlockSpec((tm, d), lambda i: (i, 0)),
        out_shape=jax.ShapeDtypeStruct((n, d), BF16),
        compiler_params=_params(("parallel",)),
        name="norm_modulate",
    )(xs, g, mod)


def _ffn_up_kernel(h_ref, wa_ref, wb_ref, o_ref, *, tm):
    def body(rows, _):
        h = h_ref[rows, :]
        a = _dot(h, wa_ref[...])
        b = _dot(h, wb_ref[...])
        o_ref[rows, :] = (a * _sigmoid(a) * b).astype(o_ref.dtype)

    _for_row_subtiles(tm, body)


def _ffn_up(h, w13, l, s):
    n, d = h.shape
    f = w13.shape[-1] // 2
    tm = _row_tile(n, DENSE_ROWS)
    tf = min(f, 512)
    nj = f // tf
    return pl.pallas_call(
        functools.partial(_ffn_up_kernel, tm=tm),
        grid=(n // tm, nj),
        in_specs=[
            _single((tm, d), lambda i, j: (i, 0)),
            pl.BlockSpec((None, None, d, tf), lambda i, j: (l, s, 0, j)),
            pl.BlockSpec((None, None, d, tf), lambda i, j: (l, s, 0, nj + j)),
        ],
        out_specs=pl.BlockSpec((tm, tf), lambda i, j: (i, j)),
        out_shape=jax.ShapeDtypeStruct((n, f), BF16),
        compiler_params=_params(("parallel", "arbitrary")),
        name="ffn_up",
    )(h, w13, w13)


def _resid_kernel(a_ref, w_ref, x_ref, gate_ref, o_ref, *, tm, seq, coef):
    row_base = pl.program_id(0) * tm

    def body(rows, r0):
        sub = tm // ROW_SPLIT
        gate = jnp.where(_is_ctx(row_base + r0, sub, seq), gate_ref[1], gate_ref[0])
        o_ref[rows, :] = x_ref[rows, :] + (coef * gate) * _dot(a_ref[rows, :], w_ref[...])

    _for_row_subtiles(tm, body)


def _resid_proj(a, w, widx, xs, gate, seq, coef):
    n, k = a.shape
    d = xs.shape[1]
    tm = _row_tile(n, DENSE_ROWS)
    tn = min(d, 256)
    lead = (None,) * len(widx)
    return pl.pallas_call(
        functools.partial(_resid_kernel, tm=tm, seq=seq, coef=coef),
        grid=(n // tm, d // tn),
        in_specs=[
            _single((tm, k), lambda i, j: (i, 0)),
            pl.BlockSpec(lead + (k, tn), lambda i, j: widx + (0, j)),
            pl.BlockSpec((tm, tn), lambda i, j: (i, j)),
            pl.BlockSpec((2, 1, tn), lambda i, j: (0, 0, j)),
        ],
        out_specs=pl.BlockSpec((tm, tn), lambda i, j: (i, j)),
        out_shape=jax.ShapeDtypeStruct((n, d), F32),
        compiler_params=_params(("parallel", "arbitrary")),
        name="gated_residual_proj",
    )(a, w, xs, gate)


def _proj_kernel(h_ref, w_ref, wg_ref, gb_ref, z_ref, zg_ref, *, tm):
    j = pl.program_id(1)

    def body(rows, _):
        h = h_ref[rows, :]
        z_ref[rows, :] = _dot(h, w_ref[...]).astype(z_ref.dtype)

        @pl.when(j == 0)
        def _():
            zg_ref[rows, :] = _dot(h, wg_ref[...]) + gb_ref[...]

    _for_row_subtiles(tm, body)


def _proj(h, w_main, w_gate, gate_b, l):
    n, d = h.shape
    nz = w_main.shape[-1]
    tm = _row_tile(n, DENSE_ROWS)
    tn = 512
    return pl.pallas_call(
        functools.partial(_proj_kernel, tm=tm),
        grid=(n // tm, nz // tn),
        in_specs=[
            _single((tm, d), lambda i, j: (i, 0)),
            pl.BlockSpec((None, d, tn), lambda i, j: (l, 0, j)),
            pl.BlockSpec((None, d, LANES), lambda i, j: (l, 0, 0)),
            pl.BlockSpec((1, LANES), lambda i, j: (0, 0)),
        ],
        out_specs=[
            pl.BlockSpec((tm, tn), lambda i, j: (i, j)),
            pl.BlockSpec((tm, LANES), lambda i, j: (i, 0)),
        ],
        out_shape=[
            jax.ShapeDtypeStruct((n, nz), BF16),
            jax.ShapeDtypeStruct((n, LANES), F32),
        ],
        compiler_params=_params(("parallel", "arbitrary")),
        name="mixer_in_proj",
    )(h, w_main, w_gate, gate_b)


def _gmlp_kernel(u_ref, v_ref, ng_ref, ws_ref, bst_ref, o_ref, *, tm):
    v = jax.nn.gelu(v_ref[...].astype(F32))
    vc = v - jnp.mean(v, axis=-1, keepdims=True)
    vn = vc * lax.rsqrt(jnp.mean(vc * vc, axis=-1, keepdims=True) + EPS) * ng_ref[...]
    vn = vn.astype(BF16)
    for c in range(tm // CHUNK):
        rows = slice(c * CHUNK, (c + 1) * CHUNK)
        for g in range(A_GROUPS):
            cols = slice(g * A_GDIM, (g + 1) * A_GDIM)
            mixed = _dot(ws_ref[g].astype(BF16), vn[rows, cols]) + bst_ref[:, g:g + 1]
            u = jax.nn.gelu(u_ref[rows, cols].astype(F32))
            o_ref[rows, cols] = (u * mixed).astype(o_ref.dtype)


def _gmlp(z, ng, ws, bst):
    n = z.shape[0]
    tm = ELEM_ROWS
    bw = BRANCH_WIDTH
    return pl.pallas_call(
        functools.partial(_gmlp_kernel, tm=tm),
        grid=(n // tm,),
        in_specs=[
            pl.BlockSpec((tm, bw), lambda i: (i, Z_UV // bw)),
            pl.BlockSpec((tm, bw), lambda i: (i, Z_UV // bw + 1)),
            pl.BlockSpec((1, bw), lambda i: (0, 0)),
            pl.BlockSpec((A_GROUPS, CHUNK, CHUNK), lambda i: (0, 0, 0)),
            pl.BlockSpec((CHUNK, A_GROUPS), lambda i: (0, 0)),
        ],
        out_specs=pl.BlockSpec((tm, bw), lambda i: (i, 0)),
        out_shape=jax.ShapeDtypeStruct((n, bw), BF16),
        compiler_params=_params(("parallel",)),
        name="gmlp_branch",
    )(z, z, ng, ws, bst)


def _mlstm_head(rev, q, k, v, gi_col, gf_col, gi_row, gf_row, c_state, n_state, m_state):
    L = q.shape[0]
    r = lax.broadcasted_iota(jnp.int32, (L, L), 0)
    c = lax.broadcasted_iota(jnp.int32, (L, L), 1)
    seen = (c >= r) if rev else (c <= r)
    seen_t = (r >= c) if rev else (r <= c)
    lf_col = _log_sigmoid(gf_col)
    lf_row = _log_sigmoid(gf_row)
    b_col = jnp.sum(jnp.where(seen, lf_row, 0.0), axis=1, keepdims=True)
    b_row = jnp.sum(jnp.where(seen_t, lf_col, 0.0), axis=0, keepdims=True)
    b_last = jnp.sum(lf_row, axis=1, keepdims=True)

    dmat = jnp.where(seen, b_col - b_row + gi_row, -jnp.inf)
    a_col = b_col + m_state
    m_t = jnp.maximum(a_col, jnp.max(dmat, axis=1, keepdims=True))
    wa = jnp.exp(a_col - m_t)
    s = _dot_nt(q, k) * jnp.exp(dmat - m_t)
    num = wa * _dot(q, c_state.astype(BF16)) + _dot(s.astype(BF16), v)
    qn = jnp.sum(q.astype(F32) * n_state, axis=1, keepdims=True)
    den = wa * qn + jnp.sum(s, axis=1, keepdims=True)
    h = num / jnp.maximum(jnp.abs(den), jnp.exp(-m_t))

    g_row = b_last - b_row + gi_row
    g_col = b_last - b_col + gi_col
    m_new = jnp.maximum(b_last + m_state, jnp.max(g_row, axis=1, keepdims=True))
    decay = jnp.exp(b_last + m_state - m_new)
    kw = k.astype(F32) * jnp.exp(g_col - m_new)
    c_new = decay * c_state + _dot(kw.T.astype(BF16), v)
    n_new = decay * n_state + jnp.sum(kw, axis=0, keepdims=True)
    return h, c_new, n_new, m_new


def _mlstm_kernel(qf_ref, kf_ref, vf_ref, gf_ref, qb_ref, kb_ref, vb_ref, gb_ref,
                  hf_ref, hb_ref, c_ref, n_ref, m_ref):
    @pl.when(pl.program_id(0) == 0)
    def _():
        c_ref[...] = jnp.zeros_like(c_ref)
        n_ref[...] = jnp.zeros_like(n_ref)
        m_ref[...] = jnp.zeros_like(m_ref)

    for d, (q_ref, k_ref, v_ref, g_ref, h_ref) in enumerate(
            ((qf_ref, kf_ref, vf_ref, gf_ref, hf_ref), (qb_ref, kb_ref, vb_ref, gb_ref, hb_ref))):
        gates = g_ref[...]
        gates_t = gates.T
        for hd in range(B_HEADS):
            idx = d * B_HEADS + hd
            ci = d * 2 * B_HEADS + hd
            cf = ci + B_HEADS
            qs = slice(hd * B_DQK, (hd + 1) * B_DQK)
            vs = slice(hd * B_DV, (hd + 1) * B_DV)
            q = (q_ref[:, qs].astype(F32) * (B_DQK ** -0.5)).astype(BF16)
            h, c_new, n_new, m_new = _mlstm_head(
                d == 1, q, k_ref[:, qs], v_ref[:, vs],
                gates[:, ci:ci + 1], gates[:, cf:cf + 1],
                gates_t[ci:ci + 1, :], gates_t[cf:cf + 1, :],
                c_ref[idx], n_ref[idx], m_ref[idx][:, 0:1])
            h_ref[:, vs] = h
            c_ref[idx] = c_new
            n_ref[idx] = n_new
            m_ref[idx] = jnp.broadcast_to(m_new, (1, LANES))


def _mlstm(z, zg, n_lat, n_ctx):
    n = z.shape[0]
    n_steps = n_lat + n_ctx
    assert n == n_steps * CHUNK

    def fwd(t):
        return (t + n_lat) % n_steps

    def bwd(t):
        return n_steps - 1 - t

    def specs(order):
        return [
            pl.BlockSpec((CHUNK, B_HEADS * B_DQK), lambda t: (order(t), Z_MQ // (B_HEADS * B_DQK))),
            pl.BlockSpec((CHUNK, B_HEADS * B_DQK), lambda t: (order(t), Z_MK // (B_HEADS * B_DQK))),
            pl.BlockSpec((CHUNK, BRANCH_WIDTH), lambda t: (order(t), Z_MV // BRANCH_WIDTH)),
            pl.BlockSpec((CHUNK, LANES), lambda t: (order(t), 0)),
        ]

    return pl.pallas_call(
        _mlstm_kernel,
        grid=(n_steps,),
        in_specs=specs(fwd) + specs(bwd),
        out_specs=[
            pl.BlockSpec((CHUNK, BRANCH_WIDTH), lambda t: (fwd(t), 0)),
            pl.BlockSpec((CHUNK, BRANCH_WIDTH), lambda t: (bwd(t), 0)),
        ],
        out_shape=[jax.ShapeDtypeStruct((n, BRANCH_WIDTH), F32)] * 2,
        scratch_shapes=[
            pltpu.VMEM((2 * B_HEADS, B_DQK, B_DV), F32),
            pltpu.VMEM((2 * B_HEADS, 1, B_DQK), F32),
            pltpu.VMEM((2 * B_HEADS, 1, LANES), F32),
        ],
        compiler_params=_params(("arbitrary",)),
        name="mlstm_bidir",
    )(z, z, z, zg, z, z, z, zg)


def _mlstm_out_kernel(hf_ref, hb_ref, og_ref, g_ref, o_ref):
    for hd in range(B_HEADS):
        cols = slice(hd * B_DV, (hd + 1) * B_DV)
        h = hf_ref[:, cols] + hb_ref[:, cols]
        hn = h * lax.rsqrt(jnp.mean(h * h, axis=-1, keepdims=True) + EPS) * g_ref[:, cols]
        o_ref[:, cols] = (_sigmoid(og_ref[:, cols].astype(F32)) * hn).astype(o_ref.dtype)


def _mlstm_out(hf, hb, z, g):
    n = z.shape[0]
    tm = ELEM_ROWS
    bw = BRANCH_WIDTH
    row = lambda i: (i, 0)
    return pl.pallas_call(
        _mlstm_out_kernel,
        grid=(n // tm,),
        in_specs=[
            pl.BlockSpec((tm, bw), row),
            pl.BlockSpec((tm, bw), row),
            pl.BlockSpec((tm, bw), lambda i: (i, Z_MO // bw)),
            pl.BlockSpec((1, bw), lambda i: (0, 0)),
        ],
        out_specs=pl.BlockSpec((tm, bw), row),
        out_shape=jax.ShapeDtypeStruct((n, bw), BF16),
        compiler_params=_params(("parallel",)),
        name="mlstm_out_gate",
    )(hf, hb, z, g)


def _rope_kernel(q_ref, k_ref, cos_ref, sin_ref, qo_ref, ko_ref):
    cos = cos_ref[...]
    sin = sin_ref[...]
    even = lax.broadcasted_iota(jnp.int32, cos.shape, 1) % 2 == 0
    for src, dst, mult in ((q_ref, qo_ref, C_DH ** -0.5 * LOG2E), (k_ref, ko_ref, None)):
        for g in range(2 * C_HEADS):
            cols = slice(g * C_DH, (g + 1) * C_DH)
            x = src[:, cols].astype(F32)
            partner = jnp.where(even, pltpu.roll(x, C_DH - 1, 1), pltpu.roll(x, 1, 1))
            y = x * cos + partner * sin
            if mult is not None:
                y = y * mult
            dst[:, cols] = y.astype(dst.dtype)


def _rope(z, cos_t, sin_t):
    n = z.shape[0]
    tm = ELEM_ROWS
    w = 2 * C_HEADS * C_DH
    return pl.pallas_call(
        _rope_kernel,
        grid=(n // tm,),
        in_specs=[
            pl.BlockSpec((tm, w), lambda i: (i, Z_AQ // w)),
            pl.BlockSpec((tm, w), lambda i: (i, Z_AK // w)),
            pl.BlockSpec((tm, C_DH), lambda i: (i, 0)),
            pl.BlockSpec((tm, C_DH), lambda i: (i, 0)),
        ],
        out_specs=[pl.BlockSpec((tm, w), lambda i: (i, 0))] * 2,
        out_shape=[jax.ShapeDtypeStruct((n, w), BF16)] * 2,
        compiler_params=_params(("parallel",)),
        name="axial_rope",
    )(z, z, cos_t, sin_t)


def _attn_kernel(q_ref, k_ref, v_ref, lam_ref, ng_ref, o_ref,
                 m1_ref, l1_ref, a1_ref, m2_ref, l2_ref, a2_ref, *, nk, lam_init):
    kk = pl.program_id(2)
    stats = ((m1_ref, l1_ref, a1_ref), (m2_ref, l2_ref, a2_ref))

    @pl.when(kk == 0)
    def _():
        for m_ref, l_ref, a_ref in stats:
            m_ref[...] = jnp.full_like(m_ref, -jnp.inf)
            l_ref[...] = jnp.zeros_like(l_ref)
            a_ref[...] = jnp.zeros_like(a_ref)

    sub = q_ref.shape[0] // ATTN_ROW_SPLIT
    for r in range(ATTN_ROW_SPLIT):
        rows = slice(r * sub, (r + 1) * sub)
        for mp_i, (m_ref, l_ref, a_ref) in enumerate(stats):
            cols = slice(mp_i * C_DH, (mp_i + 1) * C_DH)
            s = _dot_nt(q_ref[rows, cols], k_ref[:, cols])
            m_prev = m_ref[rows, :]
            m_new = jnp.maximum(m_prev, jnp.max(s, axis=-1, keepdims=True))
            alpha = jnp.exp2(m_prev - m_new)
            p = jnp.exp2(s - m_new)
            l_ref[rows, :] = alpha * l_ref[rows, :] + jnp.sum(p, axis=-1, keepdims=True)
            a_ref[rows, :] = alpha * a_ref[rows, :] + _dot(p.astype(BF16), v_ref[...])
            m_ref[rows, :] = m_new

    @pl.when(kk == nk - 1)
    def _():
        lp = lam_ref[...]
        lam = (jnp.exp(jnp.sum(lp[0:1] * lp[1:2], axis=-1, keepdims=True))
               - jnp.exp(jnp.sum(lp[2:3] * lp[3:4], axis=-1, keepdims=True)) + lam_init)
        o = a1_ref[...] / l1_ref[...] - lam * (a2_ref[...] / l2_ref[...])
        y = o * lax.rsqrt(jnp.mean(o * o, axis=-1, keepdims=True) + EPS) * ng_ref[...]
        o_ref[...] = (y * (1.0 - lam_init)).astype(o_ref.dtype)


def _attention(qr, kr, z, lam_p, ng, lam_init, q_row0, n_q_rows, kv_row0, n_kv_rows, tq, tk):
    nq = n_q_rows // tq
    nk = n_kv_rows // tk
    qb0 = q_row0 // tq
    kb0 = kv_row0 // tk
    return pl.pallas_call(
        functools.partial(_attn_kernel, nk=nk, lam_init=lam_init),
        grid=(C_HEADS, nq, nk),
        in_specs=[
            pl.BlockSpec((tq, C_DV), lambda h, i, k: (qb0 + i, h)),
            pl.BlockSpec((tk, C_DV), lambda h, i, k: (kb0 + k, h)),
            pl.BlockSpec((tk, C_DV), lambda h, i, k: (kb0 + k, Z_AV // C_DV + h)),
            pl.BlockSpec((4, C_DH), lambda h, i, k: (0, 0)),
            pl.BlockSpec((1, C_DV), lambda h, i, k: (0, 0)),
        ],
        out_specs=pl.BlockSpec((tq, C_DV), lambda h, i, k: (i, h)),
        out_shape=jax.ShapeDtypeStruct((n_q_rows, C_HEADS * C_DV), BF16),
        scratch_shapes=[
            pltpu.VMEM((tq, 1), F32), pltpu.VMEM((tq, 1), F32), pltpu.VMEM((tq, C_DV), F32),
            pltpu.VMEM((tq, 1), F32), pltpu.VMEM((tq, 1), F32), pltpu.VMEM((tq, C_DV), F32),
        ],
        compiler_params=_params(("parallel", "parallel", "arbitrary")),
        name="diff_attention",
    )(qr, kr, z, lam_p, ng)


def _merge_kernel(ya_ref, yb_ref, yc_ref, g0_ref, g1_ref, g2_ref, wb_ref, y_ref, *, tm):
    def body(rows, _):
        y = _sigmoid(g0_ref[rows, :].astype(F32)) * _dot(ya_ref[rows, :], wb_ref[0])
        y += _sigmoid(g1_ref[rows, :].astype(F32)) * _dot(yb_ref[rows, :], wb_ref[1])
        y += _sigmoid(g2_ref[rows, :].astype(F32)) * _dot(yc_ref[rows, :], wb_ref[2])
        y_ref[rows, :] = y.astype(y_ref.dtype)

    _for_row_subtiles(tm, body)


def _merge(ya, yb, yc, z, w_branch, l):
    n = z.shape[0]
    d = w_branch.shape[-1]
    tm = _row_tile(n, DENSE_ROWS)
    tn = min(d, 256)
    bw = BRANCH_WIDTH
    gate_blk = Z_GATE // tn
    row = lambda i, j: (i, 0)
    return pl.pallas_call(
        functools.partial(_merge_kernel, tm=tm),
        grid=(n // tm, d // tn),
        in_specs=[
            _single((tm, bw), row),
            _single((tm, bw), row),
            _single((tm, bw), row),
            pl.BlockSpec((tm, tn), lambda i, j: (i, gate_blk + j)),
            pl.BlockSpec((tm, tn), lambda i, j: (i, gate_blk + d // tn + j)),
            pl.BlockSpec((tm, tn), lambda i, j: (i, gate_blk + 2 * (d // tn) + j)),
            pl.BlockSpec((None, N_BRANCH, bw, tn), lambda i, j: (l, 0, 0, j)),
        ],
        out_specs=pl.BlockSpec((tm, tn), lambda i, j: (i, j)),
        out_shape=jax.ShapeDtypeStruct((n, d), BF16),
        compiler_params=_params(("parallel", "arbitrary")),
        name="branch_merge",
    )(ya, yb, yc, z, z, z, w_branch)


def _final_kernel(x_ref, g_ref, o_ref):
    x = x_ref[...]
    o_ref[...] = x * lax.rsqrt(jnp.mean(x * x, axis=-1, keepdims=True) + EPS) * g_ref[...]


def _final_norm(xs, g, n_rows):
    d = xs.shape[1]
    tm = ELEM_ROWS
    return pl.pallas_call(
        _final_kernel,
        grid=(n_rows // tm,),
        in_specs=[pl.BlockSpec((tm, d), lambda i: (i, 0)), pl.BlockSpec((1, d), lambda i: (0, 0))],
        out_specs=pl.BlockSpec((tm, d), lambda i: (i, 0)),
        out_shape=jax.ShapeDtypeStruct((n_rows, d), F32),
        compiler_params=_params(("parallel",)),
        name="final_norm",
    )(xs, g)


def _pack_w_in(w_in):
    g0 = Z_MO + BRANCH_WIDTH
    w_main = jnp.concatenate([w_in[..., :g0], w_in[..., g0 + GATE_COLS:]], axis=-1).astype(BF16)
    w_gate = jnp.pad(w_in[..., g0:g0 + GATE_COLS], ((0, 0), (0, 0), (0, LANES - GATE_COLS))).astype(BF16)
    return w_main, w_gate


def _rope_tables(n_lat, n_rows):
    t = jnp.arange(n_lat)
    r = (t // GRID_W).astype(F32)
    col = (t % GRID_W).astype(F32)
    n_freq = C_DH // 4
    inv = ROPE_THETA ** (-jnp.arange(n_freq, dtype=F32) / n_freq)
    ang = jnp.concatenate([r[:, None] * inv, col[:, None] * inv], axis=-1)
    cos = jnp.repeat(jnp.cos(ang), 2, axis=-1)
    sin = jnp.stack([-jnp.sin(ang), jnp.sin(ang)], axis=-1).reshape(n_lat, C_DH)
    pad = n_rows - n_lat
    cos = jnp.concatenate([cos, jnp.ones((pad, C_DH), F32)], axis=0)
    sin = jnp.concatenate([sin, jnp.zeros((pad, C_DH), F32)], axis=0)
    return cos, sin


def kernel(x, c, ctx, c_ctx, ada_down, ada_up, ada_bias, norm_g, ffn_w13, ffn_w2, w_in, gmlp_norm_g,
           gmlp_ws, gmlp_bs, mlstm_gate_b, mlstm_norm_g, diff_lambda, diff_norm_g, w_branch, w_out,
           final_g):
    bsz, seq, d = x.shape
    n_ctx = ctx.shape[1]
    depth = ada_down.shape[0]
    n_tok = seq + n_ctx
    assert bsz == 1 and seq % ELEM_ROWS == 0 and n_ctx % ELEM_ROWS == 0
    assert w_in.shape[-1] == Z_GATE + GATE_COLS + N_BRANCH * d

    xs = jnp.concatenate([x[0], ctx[0]], axis=0)
    w13 = ffn_w13.astype(BF16)
    w2 = ffn_w2.astype(BF16)
    w_main, w_gate = _pack_w_in(w_in)
    wbr = w_branch.astype(BF16)
    wo = w_out.astype(BF16)
    cos_t, sin_t = _rope_tables(seq, n_tok)
    gate_b = jnp.pad(mlstm_gate_b.reshape(depth, 1, GATE_COLS), ((0, 0), (0, 0), (0, LANES - GATE_COLS)))
    cv = jnp.concatenate([c, c_ctx[None], jnp.zeros((6, d), F32)], axis=0)

    mod = _modulation(cv, ada_down, ada_up, ada_bias)[:, :2].reshape(depth, 2, N_SUB, 3, d)

    tq = 1024 if seq % 1024 == 0 else ELEM_ROWS
    tk = 1408 if n_tok % 1408 == 0 else ELEM_ROWS
    for l in range(depth):
        lam_init = 0.8 - 0.6 * math.exp(-0.3 * l)

        def ffn(xs, s, sub):
            h = _normmod(xs, norm_g[l, sub:sub + 1], mod[l, :, sub], seq)
            act = _ffn_up(h, w13, l, s)
            return _resid_proj(act, w2, (l, s), xs, mod[l, :, sub, 2:3], seq, 0.5)

        xs = ffn(xs, 0, 0)

        h = _normmod(xs, norm_g[l, 1:2], mod[l, :, 1], seq)
        z, zg = _proj(h, w_main, w_gate, gate_b[l], l)
        ya = _gmlp(z, gmlp_norm_g[l][None], gmlp_ws[l], gmlp_bs[l].T)
        hf, hb = _mlstm(z, zg, seq // CHUNK, n_ctx // CHUNK)
        yb = _mlstm_out(hf, hb, z, mlstm_norm_g[l][None])
        qr, kr = _rope(z, cos_t, sin_t)
        ng = diff_norm_g[l][None]
        yc_lat = _attention(qr, kr, z, diff_lambda[l], ng, lam_init, 0, seq, 0, n_tok, tq, tk)
        yc_ctx = _attention(qr, kr, z, diff_lambda[l], ng, lam_init, seq, n_ctx, seq, n_ctx, n_ctx, n_ctx)
        yc = jnp.concatenate([yc_lat, yc_ctx], axis=0)
        y = _merge(ya, yb, yc, z, wbr, l)
        xs = _resid_proj(y, wo, (l,), xs, mod[l, :, 1, 2:3], seq, 1.0)

        xs = ffn(xs, 1, 2)
    return _final_norm(xs, final_g[None], seq)[None]
```

```python
import functools
import math

import jax
import jax.numpy as jnp
from jax import lax
from jax.experimental import pallas as pl
from jax.experimental.pallas import tpu as pltpu

F32 = jnp.float32
BF16 = jnp.bfloat16

GRID_W = 64
CHUNK = 128
BRANCH_WIDTH = 1024
A_GROUPS = 8
A_GDIM = BRANCH_WIDTH // A_GROUPS
B_HEADS = 4
B_DQK = 128
B_DV = BRANCH_WIDTH // B_HEADS
C_HEADS = 4
C_DH = 128
C_DV = 2 * C_DH
N_BRANCH = 3
N_SUB = 3
ROPE_THETA = 10000.0
EPS = 1e-6
GATE_COLS = 4 * B_HEADS
LANES = 128
Z_UV, Z_MQ, Z_MK, Z_MV, Z_MO, Z_AQ, Z_AK, Z_AV, Z_GATE = (
    0, 2048, 2560, 3072, 4096, 5120, 6144, 7168, 8192)
DENSE_ROWS = 1408
ROW_SPLIT = 2
ELEM_ROWS = 256
MLSTM_STAGES = 5
ATTN_STAGES = 5
ATTN_GROUP = 2
ATTN_ROW_SPLIT = 4
VMEM_LIMIT = 56 * 1024 * 1024
LOG2E = math.log2(math.e)


def _params(sem):
    return pltpu.CompilerParams(dimension_semantics=sem, vmem_limit_bytes=VMEM_LIMIT)


def _dot(a, b):
    return jnp.dot(a, b, preferred_element_type=F32)


def _dot_nt(a, b):
    return lax.dot_general(a, b, (((1,), (1,)), ((), ())), preferred_element_type=F32)


def _sigmoid(x):
    return 1.0 / (1.0 + jnp.exp(-x))


def _log_sigmoid(x):
    return jnp.minimum(x, 0.0) - jnp.log(1.0 + jnp.exp(-jnp.abs(x)))


def _row_tile(n, cap):
    unit = 16 * ROW_SPLIT
    return max(t for t in range(unit, min(n, cap) + 1, unit) if n % t == 0)


def _for_row_subtiles(tm, body):
    sub = tm // ROW_SPLIT

    def step(r, carry):
        body(pl.ds(pl.multiple_of(r * sub, sub), sub), r * sub)
        return carry

    lax.fori_loop(0, ROW_SPLIT, step, 0)


def _is_ctx(row0, n, seq):
    return row0 + lax.broadcasted_iota(jnp.int32, (n, 1), 0) >= seq


def _mod_kernel(cv_ref, down_ref, up_ref, b_ref, o_ref, t_ref):
    @pl.when(pl.program_id(1) == 0)
    def _():
        s = cv_ref[...]
        s = s * _sigmoid(s)
        t_ref[...] = _dot(s.astype(BF16), down_ref[...].astype(BF16))

    o_ref[...] = _dot(t_ref[...].astype(BF16), up_ref[...].astype(BF16)) + b_ref[...]


def _modulation(cv, ada_down, ada_up, ada_bias):
    depth, d, r = ada_down.shape
    n = ada_up.shape[-1]
    tn = d
    return pl.pallas_call(
        _mod_kernel,
        grid=(depth, n // tn),
        in_specs=[
            pl.BlockSpec((8, d), lambda l, j: (0, 0)),
            pl.BlockSpec((None, d, r), lambda l, j: (l, 0, 0)),
            pl.BlockSpec((None, r, tn), lambda l, j: (l, 0, j)),
            pl.BlockSpec((None, 1, tn), lambda l, j: (l, 0, j)),
        ],
        out_specs=pl.BlockSpec((None, 8, tn), lambda l, j: (l, 0, j)),
        out_shape=jax.ShapeDtypeStruct((depth, 8, n), F32),
        scratch_shapes=[pltpu.VMEM((8, r), F32)],
        compiler_params=_params(("parallel", "arbitrary")),
        name="adaln_modulation",
    )(cv, ada_down, ada_up, ada_bias.reshape(depth, 1, n))


def _normmod_kernel(x_ref, g_ref, mod_ref, h_ref, *, seq, tm):
    x = x_ref[...]
    ctx = _is_ctx(pl.program_id(0) * tm, tm, seq)
    shift = jnp.where(ctx, mod_ref[1, 0:1, :], mod_ref[0, 0:1, :])
    scale = jnp.where(ctx, mod_ref[1, 1:2, :], mod_ref[0, 1:2, :])
    y = x * lax.rsqrt(jnp.mean(x * x, axis=-1, keepdims=True) + EPS)
    h_ref[...] = ((y * g_ref[...]) * (1.0 + scale) + shift).astype(h_ref.dtype)


def _normmod(xs, g, mod, seq):
    n, d = xs.shape
    tm = ELEM_ROWS
    return pl.pallas_call(
        functools.partial(_normmod_kernel, seq=seq, tm=tm),
        grid=(n // tm,),
        in_specs=[
            pl.BlockSpec((tm, d), lambda i: (i, 0)),
            pl.BlockSpec((1, d), lambda i: (0, 0)),
            pl.BlockSpec((2, 3, d), lambda i: (0, 0, 0)),
        ],
        out_specs=pl.BlockSpec((tm, d), lambda i: (i, 0)),
        out_shape=jax.ShapeDtypeStruct((n, d), BF16),
        compiler_params=_params(("parallel",)),
        name="norm_modulate",
    )(xs, g, mod)


def _ffn_up_kernel(h_ref, wa_ref, wb_ref, o_ref, *, tm):
    def body(rows, _):
        h = h_ref[rows, :]
        a = _dot(h, wa_ref[...])
        b = _dot(h, wb_ref[...])
        o_ref[rows, :] = (a * _sigmoid(a) * b).astype(o_ref.dtype)

    _for_row_subtiles(tm, body)


def _ffn_up(h, w13, l, s):
    n, d = h.shape
    f = w13.shape[-1] // 2
    tm = _row_tile(n, DENSE_ROWS)
    tf = min(f, 512)
    nj = f // tf
    return pl.pallas_call(
        functools.partial(_ffn_up_kernel, tm=tm),
        grid=(n // tm, nj),
        in_specs=[
            pl.BlockSpec((tm, d), lambda i, j: (i, 0)),
            pl.BlockSpec((None, None, d, tf), lambda i, j: (l, s, 0, j)),
            pl.BlockSpec((None, None, d, tf), lambda i, j: (l, s, 0, nj + j)),
        ],
        out_specs=pl.BlockSpec((tm, tf), lambda i, j: (i, j)),
        out_shape=jax.ShapeDtypeStruct((n, f), BF16),
        compiler_params=_params(("parallel", "arbitrary")),
        name="ffn_up",
    )(h, w13, w13)


def _resid_kernel(a_ref, w_ref, x_ref, gate_ref, o_ref, *, tm, seq, coef):
    row_base = pl.program_id(0) * tm

    def body(rows, r0):
        sub = tm // ROW_SPLIT
        gate = jnp.where(_is_ctx(row_base + r0, sub, seq), gate_ref[1], gate_ref[0])
        o_ref[rows, :] = x_ref[rows, :] + (coef * gate) * _dot(a_ref[rows, :], w_ref[...])

    _for_row_subtiles(tm, body)


def _resid_proj(a, w, widx, xs, gate, seq, coef):
    n, k = a.shape
    d = xs.shape[1]
    tm = _row_tile(n, DENSE_ROWS)
    tn = min(d, 512)
    lead = (None,) * len(widx)
    return pl.pallas_call(
        functools.partial(_resid_kernel, tm=tm, seq=seq, coef=coef),
        grid=(n // tm, d // tn),
        in_specs=[
            pl.BlockSpec((tm, k), lambda i, j: (i, 0)),
            pl.BlockSpec(lead + (k, tn), lambda i, j: widx + (0, j)),
            pl.BlockSpec((tm, tn), lambda i, j: (i, j)),
            pl.BlockSpec((2, 1, tn), lambda i, j: (0, 0, j)),
        ],
        out_specs=pl.BlockSpec((tm, tn), lambda i, j: (i, j)),
        out_shape=jax.ShapeDtypeStruct((n, d), F32),
        compiler_params=_params(("parallel", "arbitrary")),
        name="gated_residual_proj",
    )(a, w, xs, gate)


def _proj_kernel(h_ref, w_ref, wg_ref, gb_ref, z_ref, zg_ref, *, tm):
    j = pl.program_id(1)

    def body(rows, _):
        h = h_ref[rows, :]
        z_ref[rows, :] = _dot(h, w_ref[...]).astype(z_ref.dtype)

        @pl.when(j == 0)
        def _():
            zg_ref[rows, :] = _dot(h, wg_ref[...]) + gb_ref[...]

    _for_row_subtiles(tm, body)


def _proj(h, w_main, w_gate, gate_b, l):
    n, d = h.shape
    nz = w_main.shape[-1]
    tm = _row_tile(n, DENSE_ROWS)
    tn = 1024
    return pl.pallas_call(
        functools.partial(_proj_kernel, tm=tm),
        grid=(n // tm, nz // tn),
        in_specs=[
            pl.BlockSpec((tm, d), lambda i, j: (i, 0)),
            pl.BlockSpec((None, d, tn), lambda i, j: (l, 0, j)),
            pl.BlockSpec((None, d, LANES), lambda i, j: (l, 0, 0)),
            pl.BlockSpec((1, LANES), lambda i, j: (0, 0)),
        ],
        out_specs=[
            pl.BlockSpec((tm, tn), lambda i, j: (i, j)),
            pl.BlockSpec((tm, LANES), lambda i, j: (i, 0)),
        ],
        out_shape=[
            jax.ShapeDtypeStruct((n, nz), BF16),
            jax.ShapeDtypeStruct((n, LANES), F32),
        ],
        compiler_params=_params(("parallel", "arbitrary")),
        name="mixer_in_proj",
    )(h, w_main, w_gate, gate_b)


def _gmlp_kernel(u_ref, v_ref, ng_ref, ws_ref, bst_ref, o_ref, *, tm):
    v = jax.nn.gelu(v_ref[...].astype(F32))
    vc = v - jnp.mean(v, axis=-1, keepdims=True)
    vn = vc * lax.rsqrt(jnp.mean(vc * vc, axis=-1, keepdims=True) + EPS) * ng_ref[...]
    vn = vn.astype(BF16)
    for c in range(tm // CHUNK):
        rows = slice(c * CHUNK, (c + 1) * CHUNK)
        for g in range(A_GROUPS):
            cols = slice(g * A_GDIM, (g + 1) * A_GDIM)
            mixed = _dot(ws_ref[g].astype(BF16), vn[rows, cols]) + bst_ref[:, g:g + 1]
            u = jax.nn.gelu(u_ref[rows, cols].astype(F32))
            o_ref[rows, cols] = (u * mixed).astype(o_ref.dtype)


def _gmlp(z, ng, ws, bst):
    n = z.shape[0]
    tm = ELEM_ROWS
    bw = BRANCH_WIDTH
    return pl.pallas_call(
        functools.partial(_gmlp_kernel, tm=tm),
        grid=(n // tm,),
        in_specs=[
            pl.BlockSpec((tm, bw), lambda i: (i, Z_UV // bw)),
            pl.BlockSpec((tm, bw), lambda i: (i, Z_UV // bw + 1)),
            pl.BlockSpec((1, bw), lambda i: (0, 0)),
            pl.BlockSpec((A_GROUPS, CHUNK, CHUNK), lambda i: (0, 0, 0)),
            pl.BlockSpec((CHUNK, A_GROUPS), lambda i: (0, 0)),
        ],
        out_specs=pl.BlockSpec((tm, bw), lambda i: (i, 0)),
        out_shape=jax.ShapeDtypeStruct((n, bw), BF16),
        compiler_params=_params(("parallel",)),
        name="gmlp_branch",
    )(z, z, ng, ws, bst)


def _mlstm_head(rev, q, k, v, gi_col, gf_col, gi_row, gf_row, c_state, n_state, m_state):
    L = q.shape[0]
    r = lax.broadcasted_iota(jnp.int32, (L, L), 0)
    c = lax.broadcasted_iota(jnp.int32, (L, L), 1)
    seen = (c >= r) if rev else (c <= r)
    seen_t = (r >= c) if rev else (r <= c)
    lf_col = _log_sigmoid(gf_col)
    lf_row = _log_sigmoid(gf_row)
    b_col = jnp.sum(jnp.where(seen, lf_row, 0.0), axis=1, keepdims=True)
    b_row = jnp.sum(jnp.where(seen_t, lf_col, 0.0), axis=0, keepdims=True)
    b_last = jnp.sum(lf_row, axis=1, keepdims=True)
    qk = _dot_nt(q, k)
    yield

    dmat = jnp.where(seen, b_col - b_row + gi_row, -jnp.inf)
    a_col = b_col + m_state
    m_t = jnp.maximum(a_col, jnp.max(dmat, axis=1, keepdims=True))
    g_row = b_last - b_row + gi_row
    g_col = b_last - b_col + gi_col
    m_new = jnp.maximum(b_last + m_state, jnp.max(g_row, axis=1, keepdims=True))
    yield

    wa = jnp.exp(a_col - m_t)
    s = qk * jnp.exp(dmat - m_t)
    decay = jnp.exp(b_last + m_state - m_new)
    kw = k.astype(F32) * jnp.exp(g_col - m_new)
    yield

    sv = _dot(s.astype(BF16), v)
    qc = _dot(q, c_state.astype(BF16))
    kv = _dot(kw.T.astype(BF16), v)
    yield

    qn = jnp.sum(q.astype(F32) * n_state, axis=1, keepdims=True)
    den = wa * qn + jnp.sum(s, axis=1, keepdims=True)
    h = (wa * qc + sv) / jnp.maximum(jnp.abs(den), jnp.exp(-m_t))
    c_new = decay * c_state + kv
    n_new = decay * n_state + jnp.sum(kw, axis=0, keepdims=True)
    yield h, c_new, n_new, m_new


def _mlstm_kernel(qf_ref, kf_ref, vf_ref, gf_ref, qb_ref, kb_ref, vb_ref, gb_ref,
                  hf_ref, hb_ref, c_ref, n_ref, m_ref):
    @pl.when(pl.program_id(0) == 0)
    def _():
        c_ref[...] = jnp.zeros_like(c_ref)
        n_ref[...] = jnp.zeros_like(n_ref)
        m_ref[...] = jnp.zeros_like(m_ref)

    heads = []
    for d, (q_ref, k_ref, v_ref, g_ref, h_ref) in enumerate(
            ((qf_ref, kf_ref, vf_ref, gf_ref, hf_ref), (qb_ref, kb_ref, vb_ref, gb_ref, hb_ref))):
        gates = g_ref[...]
        gates_t = gates.T
        for hd in range(B_HEADS):
            idx = d * B_HEADS + hd
            ci = d * 2 * B_HEADS + hd
            cf = ci + B_HEADS
            qs = slice(hd * B_DQK, (hd + 1) * B_DQK)
            vs = slice(hd * B_DV, (hd + 1) * B_DV)
            q = (q_ref[:, qs].astype(F32) * (B_DQK ** -0.5)).astype(BF16)
            stages = _mlstm_head(
                d == 1, q, k_ref[:, qs], v_ref[:, vs],
                gates[:, ci:ci + 1], gates[:, cf:cf + 1],
                gates_t[ci:ci + 1, :], gates_t[cf:cf + 1, :],
                c_ref[idx], n_ref[idx], m_ref[idx][:, 0:1])
            heads.append((stages, idx, h_ref, vs))

    for _ in range(MLSTM_STAGES - 1):
        for stages, _, _, _ in heads:
            next(stages)
    for stages, idx, h_ref, vs in heads:
        h, c_new, n_new, m_new = next(stages)
        h_ref[:, vs] = h
        c_ref[idx] = c_new
        n_ref[idx] = n_new
        m_ref[idx] = jnp.broadcast_to(m_new, (1, LANES))


def _mlstm(z, zg, n_lat, n_ctx):
    n = z.shape[0]
    n_steps = n_lat + n_ctx
    assert n == n_steps * CHUNK

    def fwd(t):
        return (t + n_lat) % n_steps

    def bwd(t):
        return n_steps - 1 - t

    def specs(order):
        return [
            pl.BlockSpec((CHUNK, B_HEADS * B_DQK), lambda t: (order(t), Z_MQ // (B_HEADS * B_DQK))),
            pl.BlockSpec((CHUNK, B_HEADS * B_DQK), lambda t: (order(t), Z_MK // (B_HEADS * B_DQK))),
            pl.BlockSpec((CHUNK, BRANCH_WIDTH), lambda t: (order(t), Z_MV // BRANCH_WIDTH)),
            pl.BlockSpec((CHUNK, LANES), lambda t: (order(t), 0)),
        ]

    return pl.pallas_call(
        _mlstm_kernel,
        grid=(n_steps,),
        in_specs=specs(fwd) + specs(bwd),
        out_specs=[
            pl.BlockSpec((CHUNK, BRANCH_WIDTH), lambda t: (fwd(t), 0)),
            pl.BlockSpec((CHUNK, BRANCH_WIDTH), lambda t: (bwd(t), 0)),
        ],
        out_shape=[jax.ShapeDtypeStruct((n, BRANCH_WIDTH), F32)] * 2,
        scratch_shapes=[
            pltpu.VMEM((2 * B_HEADS, B_DQK, B_DV), F32),
            pltpu.VMEM((2 * B_HEADS, 1, B_DQK), F32),
            pltpu.VMEM((2 * B_HEADS, 1, LANES), F32),
        ],
        compiler_params=_params(("arbitrary",)),
        name="mlstm_bidir",
    )(z, z, z, zg, z, z, z, zg)


def _mlstm_out_kernel(hf_ref, hb_ref, og_ref, g_ref, o_ref):
    for hd in range(B_HEADS):
        cols = slice(hd * B_DV, (hd + 1) * B_DV)
        h = hf_ref[:, cols] + hb_ref[:, cols]
        hn = h * lax.rsqrt(jnp.mean(h * h, axis=-1, keepdims=True) + EPS) * g_ref[:, cols]
        o_ref[:, cols] = (_sigmoid(og_ref[:, cols].astype(F32)) * hn).astype(o_ref.dtype)


def _mlstm_out(hf, hb, z, g):
    n = z.shape[0]
    tm = ELEM_ROWS
    bw = BRANCH_WIDTH
    row = lambda i: (i, 0)
    return pl.pallas_call(
        _mlstm_out_kernel,
        grid=(n // tm,),
        in_specs=[
            pl.BlockSpec((tm, bw), row),
            pl.BlockSpec((tm, bw), row),
            pl.BlockSpec((tm, bw), lambda i: (i, Z_MO // bw)),
            pl.BlockSpec((1, bw), lambda i: (0, 0)),
        ],
        out_specs=pl.BlockSpec((tm, bw), row),
        out_shape=jax.ShapeDtypeStruct((n, bw), BF16),
        compiler_params=_params(("parallel",)),
        name="mlstm_out_gate",
    )(hf, hb, z, g)


def _rope_kernel(q_ref, k_ref, cos_ref, sin_ref, qo_ref, ko_ref):
    cos = cos_ref[...]
    sin = sin_ref[...]
    even = lax.broadcasted_iota(jnp.int32, cos.shape, 1) % 2 == 0
    for src, dst, mult in ((q_ref, qo_ref, C_DH ** -0.5 * LOG2E), (k_ref, ko_ref, None)):
        for g in range(2 * C_HEADS):
            cols = slice(g * C_DH, (g + 1) * C_DH)
            x = src[:, cols].astype(F32)
            partner = jnp.where(even, pltpu.roll(x, C_DH - 1, 1), pltpu.roll(x, 1, 1))
            y = x * cos + partner * sin
            if mult is not None:
                y = y * mult
            dst[:, cols] = y.astype(dst.dtype)


def _rope(z, cos_t, sin_t):
    n = z.shape[0]
    tm = ELEM_ROWS
    w = 2 * C_HEADS * C_DH
    return pl.pallas_call(
        _rope_kernel,
        grid=(n // tm,),
        in_specs=[
            pl.BlockSpec((tm, w), lambda i: (i, Z_AQ // w)),
            pl.BlockSpec((tm, w), lambda i: (i, Z_AK // w)),
            pl.BlockSpec((tm, C_DH), lambda i: (i, 0)),
            pl.BlockSpec((tm, C_DH), lambda i: (i, 0)),
        ],
        out_specs=[pl.BlockSpec((tm, w), lambda i: (i, 0))] * 2,
        out_shape=[jax.ShapeDtypeStruct((n, w), BF16)] * 2,
        compiler_params=_params(("parallel",)),
        name="axial_rope",
    )(z, z, cos_t, sin_t)


def _attn_kernel(q_ref, k_ref, v_ref, lam_ref, ng_ref, o_ref,
                 m1_ref, l1_ref, a1_ref, m2_ref, l2_ref, a2_ref, *, nk, lam_init):
    kk = pl.program_id(2)
    stats = ((m1_ref, l1_ref, a1_ref), (m2_ref, l2_ref, a2_ref))

    @pl.when(kk == 0)
    def _():
        for m_ref, l_ref, a_ref in stats:
            m_ref[...] = jnp.full_like(m_ref, -jnp.inf)
            l_ref[...] = jnp.zeros_like(l_ref)
            a_ref[...] = jnp.zeros_like(a_ref)

    sub = q_ref.shape[0] // ATTN_ROW_SPLIT

    def block(r, mp_i, m_ref, l_ref, a_ref):
        rows = slice(r * sub, (r + 1) * sub)
        cols = slice(mp_i * C_DH, (mp_i + 1) * C_DH)
        s = _dot_nt(q_ref[rows, cols], k_ref[:, cols])
        yield
        m_prev = m_ref[rows, :]
        m_new = jnp.maximum(m_prev, jnp.max(s, axis=-1, keepdims=True))
        alpha = jnp.exp2(m_prev - m_new)
        m_ref[rows, :] = m_new
        yield
        p = jnp.exp2(s - m_new)
        l_ref[rows, :] = alpha * l_ref[rows, :] + jnp.sum(p, axis=-1, keepdims=True)
        yield
        pv = _dot(p.astype(BF16), v_ref[...])
        yield
        a_ref[rows, :] = alpha * a_ref[rows, :] + pv
        yield

    for r0 in range(0, ATTN_ROW_SPLIT, ATTN_GROUP):
        blocks = [block(r, mp_i, *refs) for r in range(r0, r0 + ATTN_GROUP) for mp_i, refs in enumerate(stats)]
        for _ in range(ATTN_STAGES):
            for b in blocks:
                next(b)

    @pl.when(kk == nk - 1)
    def _():
        lp = lam_ref[...]
        lam = (jnp.exp(jnp.sum(lp[0:1] * lp[1:2], axis=-1, keepdims=True))
               - jnp.exp(jnp.sum(lp[2:3] * lp[3:4], axis=-1, keepdims=True)) + lam_init)
        o = a1_ref[...] / l1_ref[...] - lam * (a2_ref[...] / l2_ref[...])
        y = o * lax.rsqrt(jnp.mean(o * o, axis=-1, keepdims=True) + EPS) * ng_ref[...]
        o_ref[...] = (y * (1.0 - lam_init)).astype(o_ref.dtype)


def _attention(qr, kr, z, lam_p, ng, lam_init, q_row0, n_q_rows, kv_row0, n_kv_rows, tq, tk):
    nq = n_q_rows // tq
    nk = n_kv_rows // tk
    qb0 = q_row0 // tq
    kb0 = kv_row0 // tk
    return pl.pallas_call(
        functools.partial(_attn_kernel, nk=nk, lam_init=lam_init),
        grid=(C_HEADS, nq, nk),
        in_specs=[
            pl.BlockSpec((tq, C_DV), lambda h, i, k: (qb0 + i, h)),
            pl.BlockSpec((tk, C_DV), lambda h, i, k: (kb0 + k, h)),
            pl.BlockSpec((tk, C_DV), lambda h, i, k: (kb0 + k, Z_AV // C_DV + h)),
            pl.BlockSpec((4, C_DH), lambda h, i, k: (0, 0)),
            pl.BlockSpec((1, C_DV), lambda h, i, k: (0, 0)),
        ],
        out_specs=pl.BlockSpec((tq, C_DV), lambda h, i, k: (i, h)),
        out_shape=jax.ShapeDtypeStruct((n_q_rows, C_HEADS * C_DV), BF16),
        scratch_shapes=[
            pltpu.VMEM((tq, 1), F32), pltpu.VMEM((tq, 1), F32), pltpu.VMEM((tq, C_DV), F32),
            pltpu.VMEM((tq, 1), F32), pltpu.VMEM((tq, 1), F32), pltpu.VMEM((tq, C_DV), F32),
        ],
        compiler_params=_params(("parallel", "parallel", "arbitrary")),
        name="diff_attention",
    )(qr, kr, z, lam_p, ng)


def _merge_kernel(ya_ref, yb_ref, yc_ref, g0_ref, g1_ref, g2_ref, wb_ref, y_ref, *, tm):
    def body(rows, _):
        y = _sigmoid(g0_ref[rows, :].astype(F32)) * _dot(ya_ref[rows, :], wb_ref[0])
        y += _sigmoid(g1_ref[rows, :].astype(F32)) * _dot(yb_ref[rows, :], wb_ref[1])
        y += _sigmoid(g2_ref[rows, :].astype(F32)) * _dot(yc_ref[rows, :], wb_ref[2])
        y_ref[rows, :] = y.astype(y_ref.dtype)

    _for_row_subtiles(tm, body)


def _merge(ya, yb, yc, z, w_branch, l):
    n = z.shape[0]
    d = w_branch.shape[-1]
    tm = _row_tile(n, DENSE_ROWS)
    tn = min(d, 512)
    bw = BRANCH_WIDTH
    gate_blk = Z_GATE // tn
    row = lambda i, j: (i, 0)
    return pl.pallas_call(
        functools.partial(_merge_kernel, tm=tm),
        grid=(n // tm, d // tn),
        in_specs=[
            pl.BlockSpec((tm, bw), row),
            pl.BlockSpec((tm, bw), row),
            pl.BlockSpec((tm, bw), row),
            pl.BlockSpec((tm, tn), lambda i, j: (i, gate_blk + j)),
            pl.BlockSpec((tm, tn), lambda i, j: (i, gate_blk + d // tn + j)),
            pl.BlockSpec((tm, tn), lambda i, j: (i, gate_blk + 2 * (d // tn) + j)),
            pl.BlockSpec((None, N_BRANCH, bw, tn), lambda i, j: (l, 0, 0, j)),
        ],
        out_specs=pl.BlockSpec((tm, tn), lambda i, j: (i, j)),
        out_shape=jax.ShapeDtypeStruct((n, d), BF16),
        compiler_params=_params(("parallel", "arbitrary")),
        name="branch_merge",
    )(ya, yb, yc, z, z, z, w_branch)


def _final_kernel(x_ref, g_ref, o_ref):
    x = x_ref[...]
    o_ref[...] = x * lax.rsqrt(jnp.mean(x * x, axis=-1, keepdims=True) + EPS) * g_ref[...]


def _final_norm(xs, g, n_rows):
    d = xs.shape[1]
    tm = ELEM_ROWS
    return pl.pallas_call(
        _final_kernel,
        grid=(n_rows // tm,),
        in_specs=[pl.BlockSpec((tm, d), lambda i: (i, 0)), pl.BlockSpec((1, d), lambda i: (0, 0))],
        out_specs=pl.BlockSpec((tm, d), lambda i: (i, 0)),
        out_shape=jax.ShapeDtypeStruct((n_rows, d), F32),
        compiler_params=_params(("parallel",)),
        name="final_norm",
    )(xs, g)


def _pack_w_in(w_in):
    g0 = Z_MO + BRANCH_WIDTH
    w_bf = w_in.astype(BF16)
    w_main = jnp.concatenate([w_bf[..., :g0], w_bf[..., g0 + GATE_COLS:]], axis=-1)
    w_gate = jnp.pad(w_bf[..., g0:g0 + GATE_COLS], ((0, 0), (0, 0), (0, LANES - GATE_COLS)))
    return w_main, w_gate


def _rope_tables(n_lat, n_rows):
    t = jnp.arange(n_lat)
    r = (t // GRID_W).astype(F32)
    col = (t % GRID_W).astype(F32)
    n_freq = C_DH // 4
    inv = ROPE_THETA ** (-jnp.arange(n_freq, dtype=F32) / n_freq)
    ang = jnp.concatenate([r[:, None] * inv, col[:, None] * inv], axis=-1)
    cos = jnp.repeat(jnp.cos(ang), 2, axis=-1)
    sin = jnp.stack([-jnp.sin(ang), jnp.sin(ang)], axis=-1).reshape(n_lat, C_DH)
    pad = n_rows - n_lat
    cos = jnp.concatenate([cos, jnp.ones((pad, C_DH), F32)], axis=0)
    sin = jnp.concatenate([sin, jnp.zeros((pad, C_DH), F32)], axis=0)
    return cos, sin


def kernel(x, c, ctx, c_ctx, ada_down, ada_up, ada_bias, norm_g, ffn_w13, ffn_w2, w_in, gmlp_norm_g,
           gmlp_ws, gmlp_bs, mlstm_gate_b, mlstm_norm_g, diff_lambda, diff_norm_g, w_branch, w_out,
           final_g):
    bsz, seq, d = x.shape
    n_ctx = ctx.shape[1]
    depth = ada_down.shape[0]
    n_tok = seq + n_ctx
    assert bsz == 1 and seq % ELEM_ROWS == 0 and n_ctx % ELEM_ROWS == 0
    assert w_in.shape[-1] == Z_GATE + GATE_COLS + N_BRANCH * d

    xs = jnp.concatenate([x[0], ctx[0]], axis=0)
    w13 = ffn_w13.astype(BF16)
    w2 = ffn_w2.astype(BF16)
    w_main, w_gate = _pack_w_in(w_in)
    wbr = w_branch.astype(BF16)
    wo = w_out.astype(BF16)
    cos_t, sin_t = _rope_tables(seq, n_tok)
    gate_b = jnp.pad(mlstm_gate_b.reshape(depth, 1, GATE_COLS), ((0, 0), (0, 0), (0, LANES - GATE_COLS)))
    cv = jnp.concatenate([c, c_ctx[None], jnp.zeros((6, d), F32)], axis=0)

    mod = _modulation(cv, ada_down, ada_up, ada_bias)[:, :2].reshape(depth, 2, N_SUB, 3, d)

    tq = 1024 if seq % 1024 == 0 else ELEM_ROWS
    tk = 1408 if n_tok % 1408 == 0 else ELEM_ROWS
    for l in range(depth):
        lam_init = 0.8 - 0.6 * math.exp(-0.3 * l)

        def ffn(xs, s, sub):
            h = _normmod(xs, norm_g[l, sub:sub + 1], mod[l, :, sub], seq)
            act = _ffn_up(h, w13, l, s)
            return _resid_proj(act, w2, (l, s), xs, mod[l, :, sub, 2:3], seq, 0.5)

        xs = ffn(xs, 0, 0)

        h = _normmod(xs, norm_g[l, 1:2], mod[l, :, 1], seq)
        z, zg = _proj(h, w_main, w_gate, gate_b[l], l)
        ya = _gmlp(z, gmlp_norm_g[l][None], gmlp_ws[l], gmlp_bs[l].T)
        hf, hb = _mlstm(z, zg, seq // CHUNK, n_ctx // CHUNK)
        yb = _mlstm_out(hf, hb, z, mlstm_norm_g[l][None])
        qr, kr = _rope(z, cos_t, sin_t)
        ng = diff_norm_g[l][None]
        yc_lat = _attention(qr, kr, z, diff_lambda[l], ng, lam_init, 0, seq, 0, n_tok, tq, tk)
        yc_ctx = _attention(qr, kr, z, diff_lambda[l], ng, lam_init, seq, n_ctx, seq, n_ctx, n_ctx, n_ctx)
        yc = jnp.concatenate([yc_lat, yc_ctx], axis=0)
        y = _merge(ya, yb, yc, z, wbr, l)
        xs = _resid_proj(y, wo, (l,), xs, mod[l, :, 1, 2:3], seq, 1.0)

        xs = ffn(xs, 1, 2)
    return _final_norm(xs, final_g[None], seq)[None]
```

```python
import functools
import math

import jax
import jax.numpy as jnp
from jax import lax
from jax.experimental import pallas as pl
from jax.experimental.pallas import tpu as pltpu

F32 = jnp.float32
BF16 = jnp.bfloat16

GRID_W = 64
CHUNK = 128
BRANCH_WIDTH = 1024
A_GROUPS = 8
A_GDIM = BRANCH_WIDTH // A_GROUPS
B_HEADS = 4
B_DQK = 128
B_DV = BRANCH_WIDTH // B_HEADS
C_HEADS = 4
C_DH = 128
C_DV = 2 * C_DH
N_BRANCH = 3
N_SUB = 3
ROPE_THETA = 10000.0
EPS = 1e-6
GATE_COLS = 4 * B_HEADS
LANES = 128
Z_UV, Z_MQ, Z_MK, Z_MV, Z_MO, Z_AQ, Z_AK, Z_AV, Z_GATE = (
    0, 2048, 2560, 3072, 4096, 5120, 6144, 7168, 8192)
DENSE_ROWS = 1408
ROW_SPLIT = 2
ELEM_ROWS = 256
MLSTM_STAGES = 5
ATTN_STAGES = 5
ATTN_GROUP = 2
ATTN_ROW_SPLIT = 4
VMEM_LIMIT = 56 * 1024 * 1024
LOG2E = math.log2(math.e)


def _params(sem):
    return pltpu.CompilerParams(dimension_semantics=sem, vmem_limit_bytes=VMEM_LIMIT)


def _dot(a, b):
    return jnp.dot(a, b, preferred_element_type=F32)


def _dot_nt(a, b):
    return lax.dot_general(a, b, (((1,), (1,)), ((), ())), preferred_element_type=F32)


def _sigmoid(x):
    return 0.5 * jnp.tanh(0.5 * x) + 0.5


def _log_sigmoid(x):
    return jnp.minimum(x, 0.0) - jnp.log(1.0 + jnp.exp(-jnp.abs(x)))


def _row_tile(n, cap):
    unit = 16 * ROW_SPLIT
    return max(t for t in range(unit, min(n, cap) + 1, unit) if n % t == 0)


def _for_row_subtiles(tm, body):
    sub = tm // ROW_SPLIT

    def step(r, carry):
        body(pl.ds(pl.multiple_of(r * sub, sub), sub), r * sub)
        return carry

    lax.fori_loop(0, ROW_SPLIT, step, 0)


def _is_ctx(row0, n, seq):
    return row0 + lax.broadcasted_iota(jnp.int32, (n, 1), 0) >= seq


def _mod_kernel(cv_ref, down_ref, up_ref, b_ref, o_ref, t_ref):
    @pl.when(pl.program_id(1) == 0)
    def _():
        s = cv_ref[...]
        s = s * _sigmoid(s)
        t_ref[...] = _dot(s.astype(BF16), down_ref[...].astype(BF16))

    o_ref[...] = _dot(t_ref[...].astype(BF16), up_ref[...].astype(BF16)) + b_ref[...]


def _modulation(cv, ada_down, ada_up, ada_bias):
    depth, d, r = ada_down.shape
    n = ada_up.shape[-1]
    tn = 3 * d
    return pl.pallas_call(
        _mod_kernel,
        grid=(depth, n // tn),
        in_specs=[
            pl.BlockSpec((8, d), lambda l, j: (0, 0)),
            pl.BlockSpec((None, d, r), lambda l, j: (l, 0, 0)),
            pl.BlockSpec((None, r, tn), lambda l, j: (l, 0, j)),
            pl.BlockSpec((None, 1, tn), lambda l, j: (l, 0, j)),
        ],
        out_specs=pl.BlockSpec((None, 8, tn), lambda l, j: (l, 0, j)),
        out_shape=jax.ShapeDtypeStruct((depth, 8, n), F32),
        scratch_shapes=[pltpu.VMEM((8, r), F32)],
        compiler_params=_params(("parallel", "arbitrary")),
        name="adaln_modulation",
    )(cv, ada_down, ada_up, ada_bias.reshape(depth, 1, n))


def _normmod_kernel(x_ref, g_ref, mod_ref, h_ref, *, seq, tm):
    x = x_ref[...]
    ctx = _is_ctx(pl.program_id(0) * tm, tm, seq)
    shift = jnp.where(ctx, mod_ref[1, 0:1, :], mod_ref[0, 0:1, :])
    scale = jnp.where(ctx, mod_ref[1, 1:2, :], mod_ref[0, 1:2, :])
    y = x * lax.rsqrt(jnp.mean(x * x, axis=-1, keepdims=True) + EPS)
    h_ref[...] = ((y * g_ref[...]) * (1.0 + scale) + shift).astype(h_ref.dtype)


def _normmod(xs, g, mod, seq, tm, x_buffers):
    n, d = xs.shape
    return pl.pallas_call(
        functools.partial(_normmod_kernel, seq=seq, tm=tm),
        grid=(n // tm,),
        in_specs=[
            pl.BlockSpec((tm, d), lambda i: (i, 0), pipeline_mode=pl.Buffered(x_buffers)),
            pl.BlockSpec((1, d), lambda i: (0, 0)),
            pl.BlockSpec((2, 3, d), lambda i: (0, 0, 0)),
        ],
        out_specs=pl.BlockSpec((tm, d), lambda i: (i, 0)),
        out_shape=jax.ShapeDtypeStruct((n, d), BF16),
        compiler_params=_params(("parallel",)),
        name="norm_modulate",
    )(xs, g, mod)


def _ffn_up_kernel(h_ref, wa_ref, wb_ref, o_ref, *, tm):
    def body(rows, _):
        h = h_ref[rows, :]
        a = _dot(h, wa_ref[...])
        b = _dot(h, wb_ref[...])
        o_ref[rows, :] = (a * _sigmoid(a) * b).astype(o_ref.dtype)

    _for_row_subtiles(tm, body)


def _ffn_up(h, w13, l, s):
    n, d = h.shape
    f = w13.shape[-1] // 2
    tm = _row_tile(n, DENSE_ROWS)
    tf = min(f, 512)
    nj = f // tf
    return pl.pallas_call(
        functools.partial(_ffn_up_kernel, tm=tm),
        grid=(n // tm, nj),
        in_specs=[
            pl.BlockSpec((tm, d), lambda i, j: (i, 0)),
            pl.BlockSpec((None, None, d, tf), lambda i, j: (l, s, 0, j)),
            pl.BlockSpec((None, None, d, tf), lambda i, j: (l, s, 0, nj + j)),
        ],
        out_specs=pl.BlockSpec((tm, tf), lambda i, j: (i, j)),
        out_shape=jax.ShapeDtypeStruct((n, f), BF16),
        compiler_params=_params(("parallel", "arbitrary")),
        name="ffn_up",
    )(h, w13, w13)


def _resid_kernel(a_ref, w_ref, x_ref, gate_ref, o_ref, *, tm, seq, coef):
    row_base = pl.program_id(0) * tm

    def body(rows, r0):
        sub = tm // ROW_SPLIT
        gate = jnp.where(_is_ctx(row_base + r0, sub, seq), gate_ref[1], gate_ref[0])
        o_ref[rows, :] = x_ref[rows, :] + (coef * gate) * _dot(a_ref[rows, :], w_ref[...])

    _for_row_subtiles(tm, body)


def _resid_proj(a, w, widx, xs, gate, seq, coef):
    n, d = xs.shape
    k = a.shape[1]
    tm = _row_tile(n, DENSE_ROWS)
    tn = min(d, 512)
    lead = (None,) * len(widx)
    return pl.pallas_call(
        functools.partial(_resid_kernel, tm=tm, seq=seq, coef=coef),
        grid=(n // tm, d // tn),
        in_specs=[
            pl.BlockSpec((tm, k), lambda i, j: (i, 0)),
            pl.BlockSpec(lead + (k, tn), lambda i, j: widx + (0, j)),
            pl.BlockSpec((tm, tn), lambda i, j: (i, j)),
            pl.BlockSpec((2, 1, tn), lambda i, j: (0, 0, j)),
        ],
        out_specs=pl.BlockSpec((tm, tn), lambda i, j: (i, j)),
        out_shape=jax.ShapeDtypeStruct((n, d), F32),
        compiler_params=_params(("parallel", "arbitrary")),
        name="gated_residual_proj",
    )(a, w, xs, gate)


def _proj_kernel(h_ref, wt_ref, wgt_ref, gb_ref, z_ref, zg_ref, *, tm):
    j = pl.program_id(1)

    def body(rows, _):
        h = h_ref[rows, :]
        z_ref[rows, :] = _dot_nt(h, wt_ref[...]).astype(z_ref.dtype)

        @pl.when(j == 0)
        def _():
            zg_ref[rows, :] = _dot_nt(h, wgt_ref[...]) + gb_ref[...]

    _for_row_subtiles(tm, body)


def _proj(h, w_main_t, w_gate_t, gate_b, l):
    n, d = h.shape
    nz = w_main_t.shape[1]
    tm = _row_tile(n, DENSE_ROWS)
    tn = 1024
    return pl.pallas_call(
        functools.partial(_proj_kernel, tm=tm),
        grid=(n // tm, nz // tn),
        in_specs=[
            pl.BlockSpec((tm, d), lambda i, j: (i, 0)),
            pl.BlockSpec((None, tn, d), lambda i, j: (l, j, 0)),
            pl.BlockSpec((None, LANES, d), lambda i, j: (l, 0, 0)),
            pl.BlockSpec((1, LANES), lambda i, j: (0, 0)),
        ],
        out_specs=[
            pl.BlockSpec((tm, tn), lambda i, j: (i, j)),
            pl.BlockSpec((tm, LANES), lambda i, j: (i, 0)),
        ],
        out_shape=[
            jax.ShapeDtypeStruct((n, nz), BF16),
            jax.ShapeDtypeStruct((n, LANES), F32),
        ],
        compiler_params=_params(("parallel", "arbitrary")),
        name="mixer_in_proj",
    )(h, w_main_t, w_gate_t, gate_b)


def _gmlp_kernel(u_ref, v_ref, ng_ref, ws_ref, bst_ref, o_ref, *, tm):
    v = jax.nn.gelu(v_ref[...].astype(F32))
    vc = v - jnp.mean(v, axis=-1, keepdims=True)
    vn = vc * lax.rsqrt(jnp.mean(vc * vc, axis=-1, keepdims=True) + EPS) * ng_ref[...]
    vn = vn.astype(BF16)
    for c in range(tm // CHUNK):
        rows = slice(c * CHUNK, (c + 1) * CHUNK)
        for g in range(A_GROUPS):
            cols = slice(g * A_GDIM, (g + 1) * A_GDIM)
            mixed = _dot(ws_ref[g].astype(BF16), vn[rows, cols]) + bst_ref[:, g:g + 1]
            u = jax.nn.gelu(u_ref[rows, cols].astype(F32))
            o_ref[rows, cols] = (u * mixed).astype(o_ref.dtype)


def _gmlp(z, n, ng, ws, bst):
    tm = ELEM_ROWS
    bw = BRANCH_WIDTH
    return pl.pallas_call(
        functools.partial(_gmlp_kernel, tm=tm),
        grid=(n // tm,),
        in_specs=[
            pl.BlockSpec((tm, bw), lambda i: (i, Z_UV // bw)),
            pl.BlockSpec((tm, bw), lambda i: (i, Z_UV // bw + 1)),
            pl.BlockSpec((1, bw), lambda i: (0, 0)),
            pl.BlockSpec((A_GROUPS, CHUNK, CHUNK), lambda i: (0, 0, 0)),
            pl.BlockSpec((CHUNK, A_GROUPS), lambda i: (0, 0)),
        ],
        out_specs=pl.BlockSpec((tm, bw), lambda i: (i, 0)),
        out_shape=jax.ShapeDtypeStruct((n, bw), BF16),
        compiler_params=_params(("parallel",)),
        name="gmlp_branch",
    )(z, z, ng, ws, bst)


def _mlstm_head(rev, q, k, v, gi_col, gf_col, gi_row, gf_row, c_state, n_state, m_state):
    L = q.shape[0]
    r = lax.broadcasted_iota(jnp.int32, (L, L), 0)
    c = lax.broadcasted_iota(jnp.int32, (L, L), 1)
    seen = (c >= r) if rev else (c <= r)
    seen_t = (r >= c) if rev else (r <= c)
    lf_col = _log_sigmoid(gf_col)
    lf_row = _log_sigmoid(gf_row)
    b_col = jnp.sum(jnp.where(seen, lf_row, 0.0), axis=1, keepdims=True)
    b_row = jnp.sum(jnp.where(seen_t, lf_col, 0.0), axis=0, keepdims=True)
    b_last = jnp.sum(lf_row, axis=1, keepdims=True)
    qk = _dot_nt(q, k)
    yield

    dmat = jnp.where(seen, b_col - b_row + gi_row, -jnp.inf)
    a_col = b_col + m_state
    m_t = jnp.maximum(a_col, jnp.max(dmat, axis=1, keepdims=True))
    g_row = b_last - b_row + gi_row
    g_col = b_last - b_col + gi_col
    m_new = jnp.maximum(b_last + m_state, jnp.max(g_row, axis=1, keepdims=True))
    yield

    wa = jnp.exp(a_col - m_t)
    s = qk * jnp.exp(dmat - m_t)
    decay = jnp.exp(b_last + m_state - m_new)
    kw = k.astype(F32) * jnp.exp(g_col - m_new)
    yield

    sv = _dot(s.astype(BF16), v)
    qc = _dot(q, c_state.astype(BF16))
    kv = _dot(kw.T.astype(BF16), v)
    yield

    qn = jnp.sum(q.astype(F32) * n_state, axis=1, keepdims=True)
    den = wa * qn + jnp.sum(s, axis=1, keepdims=True)
    h = (wa * qc + sv) / jnp.maximum(jnp.abs(den), jnp.exp(-m_t))
    c_new = decay * c_state + kv
    n_new = decay * n_state + jnp.sum(kw, axis=0, keepdims=True)
    yield h, c_new, n_new, m_new


def _mlstm_kernel(qf_ref, kf_ref, vf_ref, gf_ref, qb_ref, kb_ref, vb_ref, gb_ref,
                  hf_ref, hb_ref, c_ref, n_ref, m_ref):
    @pl.when(pl.program_id(0) == 0)
    def _():
        c_ref[...] = jnp.zeros_like(c_ref)
        n_ref[...] = jnp.zeros_like(n_ref)
        m_ref[...] = jnp.zeros_like(m_ref)

    heads = []
    for d, (q_ref, k_ref, v_ref, g_ref, h_ref) in enumerate(
            ((qf_ref, kf_ref, vf_ref, gf_ref, hf_ref), (qb_ref, kb_ref, vb_ref, gb_ref, hb_ref))):
        gates = g_ref[...]
        gates_t = gates.T
        for hd in range(B_HEADS):
            idx = d * B_HEADS + hd
            ci = d * 2 * B_HEADS + hd
            cf = ci + B_HEADS
            qs = slice(hd * B_DQK, (hd + 1) * B_DQK)
            vs = slice(hd * B_DV, (hd + 1) * B_DV)
            q = (q_ref[:, qs].astype(F32) * (B_DQK ** -0.5)).astype(BF16)
            stages = _mlstm_head(
                d == 1, q, k_ref[:, qs], v_ref[:, vs],
                gates[:, ci:ci + 1], gates[:, cf:cf + 1],
                gates_t[ci:ci + 1, :], gates_t[cf:cf + 1, :],
                c_ref[idx], n_ref[idx], m_ref[idx][:, 0:1])
            heads.append((stages, idx, h_ref, vs))

    for _ in range(MLSTM_STAGES - 1):
        for stages, _, _, _ in heads:
            next(stages)
    for stages, idx, h_ref, vs in heads:
        h, c_new, n_new, m_new = next(stages)
        h_ref[:, vs] = h
        c_ref[idx] = c_new
        n_ref[idx] = n_new
        m_ref[idx] = jnp.broadcast_to(m_new, (1, LANES))


def _mlstm(z, zg, n_lat, n_ctx):
    n_steps = n_lat + n_ctx
    n = n_steps * CHUNK

    def fwd(t):
        return (t + n_lat) % n_steps

    def bwd(t):
        return n_steps - 1 - t

    def specs(order):
        return [
            pl.BlockSpec((CHUNK, B_HEADS * B_DQK), lambda t: (order(t), Z_MQ // (B_HEADS * B_DQK))),
            pl.BlockSpec((CHUNK, B_HEADS * B_DQK), lambda t: (order(t), Z_MK // (B_HEADS * B_DQK))),
            pl.BlockSpec((CHUNK, BRANCH_WIDTH), lambda t: (order(t), Z_MV // BRANCH_WIDTH)),
            pl.BlockSpec((CHUNK, LANES), lambda t: (order(t), 0)),
        ]

    return pl.pallas_call(
        _mlstm_kernel,
        grid=(n_steps,),
        in_specs=specs(fwd) + specs(bwd),
        out_specs=[
            pl.BlockSpec((CHUNK, BRANCH_WIDTH), lambda t: (fwd(t), 0)),
            pl.BlockSpec((CHUNK, BRANCH_WIDTH), lambda t: (bwd(t), 0)),
        ],
        out_shape=[jax.ShapeDtypeStruct((n, BRANCH_WIDTH), F32)] * 2,
        scratch_shapes=[
            pltpu.VMEM((2 * B_HEADS, B_DQK, B_DV), F32),
            pltpu.VMEM((2 * B_HEADS, 1, B_DQK), F32),
            pltpu.VMEM((2 * B_HEADS, 1, LANES), F32),
        ],
        compiler_params=_params(("arbitrary",)),
        name="mlstm_bidir",
    )(z, z, z, zg, z, z, z, zg)


def _mlstm_out_kernel(hf_ref, hb_ref, og_ref, g_ref, o_ref):
    for hd in range(B_HEADS):
        cols = slice(hd * B_DV, (hd + 1) * B_DV)
        h = hf_ref[:, cols] + hb_ref[:, cols]
        hn = h * lax.rsqrt(jnp.mean(h * h, axis=-1, keepdims=True) + EPS) * g_ref[:, cols]
        o_ref[:, cols] = (_sigmoid(og_ref[:, cols].astype(F32)) * hn).astype(o_ref.dtype)


def _mlstm_out(hf, hb, z, g):
    n = hf.shape[0]
    tm = ELEM_ROWS
    bw = BRANCH_WIDTH
    row = lambda i: (i, 0)
    return pl.pallas_call(
        _mlstm_out_kernel,
        grid=(n // tm,),
        in_specs=[
            pl.BlockSpec((tm, bw), row),
            pl.BlockSpec((tm, bw), row),
            pl.BlockSpec((tm, bw), lambda i: (i, Z_MO // bw)),
            pl.BlockSpec((1, bw), lambda i: (0, 0)),
        ],
        out_specs=pl.BlockSpec((tm, bw), row),
        out_shape=jax.ShapeDtypeStruct((n, bw), BF16),
        compiler_params=_params(("parallel",)),
        name="mlstm_out_gate",
    )(hf, hb, z, g)


def _rope_kernel(q_ref, k_ref, cos_ref, sin_ref, qo_ref, ko_ref):
    cos = cos_ref[...]
    sin = sin_ref[...]
    even = lax.broadcasted_iota(jnp.int32, cos.shape, 1) % 2 == 0
    for src, dst, mult in ((q_ref, qo_ref, C_DH ** -0.5 * LOG2E), (k_ref, ko_ref, None)):
        for g in range(2 * C_HEADS):
            cols = slice(g * C_DH, (g + 1) * C_DH)
            x = src[:, cols].astype(F32)
            partner = jnp.where(even, pltpu.roll(x, C_DH - 1, 1), pltpu.roll(x, 1, 1))
            y = x * cos + partner * sin
            if mult is not None:
                y = y * mult
            dst[:, cols] = y.astype(dst.dtype)


def _rope(z, cos_t, sin_t):
    n = cos_t.shape[0]
    tm = ELEM_ROWS
    w = 2 * C_HEADS * C_DH
    return pl.pallas_call(
        _rope_kernel,
        grid=(n // tm,),
        in_specs=[
            pl.BlockSpec((tm, w), lambda i: (i, Z_AQ // w)),
            pl.BlockSpec((tm, w), lambda i: (i, Z_AK // w)),
            pl.BlockSpec((tm, C_DH), lambda i: (i, 0)),
            pl.BlockSpec((tm, C_DH), lambda i: (i, 0)),
        ],
        out_specs=[pl.BlockSpec((tm, w), lambda i: (i, 0))] * 2,
        out_shape=[jax.ShapeDtypeStruct((n, w), BF16)] * 2,
        compiler_params=_params(("parallel",)),
        name="axial_rope",
    )(z, z, cos_t, sin_t)


def _attn_kernel(q_ref, k_ref, v_ref, lam_ref, ng_ref, o_ref,
                 m1_ref, l1_ref, a1_ref, m2_ref, l2_ref, a2_ref, *, nk, lam_init):
    kk = pl.program_id(2)
    stats = ((m1_ref, l1_ref, a1_ref), (m2_ref, l2_ref, a2_ref))

    @pl.when(kk == 0)
    def _():
        for m_ref, l_ref, a_ref in stats:
            m_ref[...] = jnp.full_like(m_ref, -jnp.inf)
            l_ref[...] = jnp.zeros_like(l_ref)
            a_ref[...] = jnp.zeros_like(a_ref)

    sub = q_ref.shape[0] // ATTN_ROW_SPLIT

    def block(r, mp_i, m_ref, l_ref, a_ref):
        rows = slice(r * sub, (r + 1) * sub)
        cols = slice(mp_i * C_DH, (mp_i + 1) * C_DH)
        s = _dot_nt(q_ref[rows, cols], k_ref[:, cols])
        yield
        m_prev = m_ref[rows, :]
        m_new = jnp.maximum(m_prev, jnp.max(s, axis=-1, keepdims=True))
        alpha = jnp.exp2(m_prev - m_new)
        m_ref[rows, :] = m_new
        yield
        p = jnp.exp2(s - m_new)
        l_ref[rows, :] = alpha * l_ref[rows, :] + jnp.sum(p, axis=-1, keepdims=True)
        yield
        pv = _dot(p.astype(BF16), v_ref[...])
        yield
        a_ref[rows, :] = alpha * a_ref[rows, :] + pv
        yield

    for r0 in range(0, ATTN_ROW_SPLIT, ATTN_GROUP):
        blocks = [block(r, mp_i, *refs) for r in range(r0, r0 + ATTN_GROUP) for mp_i, refs in enumerate(stats)]
        for _ in range(ATTN_STAGES):
            for b in blocks:
                next(b)

    @pl.when(kk == nk - 1)
    def _():
        lp = lam_ref[...]
        lam = (jnp.exp(jnp.sum(lp[0:1] * lp[1:2], axis=-1, keepdims=True))
               - jnp.exp(jnp.sum(lp[2:3] * lp[3:4], axis=-1, keepdims=True)) + lam_init)
        o = a1_ref[...] / l1_ref[...] - lam * (a2_ref[...] / l2_ref[...])
        y = o * lax.rsqrt(jnp.mean(o * o, axis=-1, keepdims=True) + EPS) * ng_ref[...]
        o_ref[...] = (y * (1.0 - lam_init)).astype(o_ref.dtype)


def _attention(qr, kr, z, lam_p, ng, lam_init, q_row0, n_q_rows, kv_row0, n_kv_rows, tq, tk):
    nq = n_q_rows // tq
    nk = n_kv_rows // tk
    qb0 = q_row0 // tq
    kb0 = kv_row0 // tk
    return pl.pallas_call(
        functools.partial(_attn_kernel, nk=nk, lam_init=lam_init),
        grid=(C_HEADS, nq, nk),
        in_specs=[
            pl.BlockSpec((tq, C_DV), lambda h, i, k: (qb0 + i, h)),
            pl.BlockSpec((tk, C_DV), lambda h, i, k: (kb0 + k, h)),
            pl.BlockSpec((tk, C_DV), lambda h, i, k: (kb0 + k, Z_AV // C_DV + h)),
            pl.BlockSpec((4, C_DH), lambda h, i, k: (0, 0)),
            pl.BlockSpec((1, C_DV), lambda h, i, k: (0, 0)),
        ],
        out_specs=pl.BlockSpec((tq, C_DV), lambda h, i, k: (i, h)),
        out_shape=jax.ShapeDtypeStruct((n_q_rows, C_HEADS * C_DV), BF16),
        scratch_shapes=[
            pltpu.VMEM((tq, 1), F32), pltpu.VMEM((tq, 1), F32), pltpu.VMEM((tq, C_DV), F32),
            pltpu.VMEM((tq, 1), F32), pltpu.VMEM((tq, 1), F32), pltpu.VMEM((tq, C_DV), F32),
        ],
        compiler_params=_params(("parallel", "parallel", "arbitrary")),
        name="diff_attention",
    )(qr, kr, z, lam_p, ng)


def _merge_kernel(ya_ref, yb_ref, yc_ref, g0_ref, g1_ref, g2_ref, wb_ref, y_ref, *, tm):
    def body(rows, _):
        y = _sigmoid(g0_ref[rows, :].astype(F32)) * _dot(ya_ref[rows, :], wb_ref[0])
        y += _sigmoid(g1_ref[rows, :].astype(F32)) * _dot(yb_ref[rows, :], wb_ref[1])
        y += _sigmoid(g2_ref[rows, :].astype(F32)) * _dot(yc_ref[rows, :], wb_ref[2])
        y_ref[rows, :] = y.astype(y_ref.dtype)

    _for_row_subtiles(tm, body)


def _merge(ya, yb, yc, z, w_branch, l):
    n = ya.shape[0]
    d = w_branch.shape[-1]
    tm = _row_tile(n, DENSE_ROWS)
    tn = min(d, 512)
    bw = BRANCH_WIDTH
    gate_blk = Z_GATE // tn
    row = lambda i, j: (i, 0)
    return pl.pallas_call(
        functools.partial(_merge_kernel, tm=tm),
        grid=(n // tm, d // tn),
        in_specs=[
            pl.BlockSpec((tm, bw), row),
            pl.BlockSpec((tm, bw), row),
            pl.BlockSpec((tm, bw), row),
            pl.BlockSpec((tm, tn), lambda i, j: (i, gate_blk + j)),
            pl.BlockSpec((tm, tn), lambda i, j: (i, gate_blk + d // tn + j)),
            pl.BlockSpec((tm, tn), lambda i, j: (i, gate_blk + 2 * (d // tn) + j)),
            pl.BlockSpec((None, N_BRANCH, bw, tn), lambda i, j: (l, 0, 0, j)),
        ],
        out_specs=pl.BlockSpec((tm, tn), lambda i, j: (i, j)),
        out_shape=jax.ShapeDtypeStruct((n, d), BF16),
        compiler_params=_params(("parallel", "arbitrary")),
        name="branch_merge",
    )(ya, yb, yc, z, z, z, w_branch)


def _final_kernel(x_ref, g_ref, o_ref):
    x = x_ref[...]
    o_ref[...] = x * lax.rsqrt(jnp.mean(x * x, axis=-1, keepdims=True) + EPS) * g_ref[...]


def _final_norm(xs, g, n_rows):
    d = xs.shape[1]
    tm = ELEM_ROWS
    return pl.pallas_call(
        _final_kernel,
        grid=(n_rows // tm,),
        in_specs=[pl.BlockSpec((tm, d), lambda i: (i, 0)), pl.BlockSpec((1, d), lambda i: (0, 0))],
        out_specs=pl.BlockSpec((tm, d), lambda i: (i, 0)),
        out_shape=jax.ShapeDtypeStruct((n_rows, d), F32),
        compiler_params=_params(("parallel",)),
        name="final_norm",
    )(xs, g)


def _pack_w_in(w_in):
    g0 = Z_MO + BRANCH_WIDTH
    w_t = jnp.swapaxes(w_in, 1, 2)
    w_main_t = jnp.concatenate([w_t[:, :g0], w_t[:, g0 + GATE_COLS:]], axis=1).astype(BF16)
    w_gate_t = jnp.pad(w_t[:, g0:g0 + GATE_COLS], ((0, 0), (0, LANES - GATE_COLS), (0, 0))).astype(BF16)
    return w_main_t, w_gate_t


def _rope_tables(n_lat, n_rows):
    t = jnp.arange(n_lat)
    r = (t // GRID_W).astype(F32)
    col = (t % GRID_W).astype(F32)
    n_freq = C_DH // 4
    inv = ROPE_THETA ** (-jnp.arange(n_freq, dtype=F32) / n_freq)
    ang = jnp.concatenate([r[:, None] * inv, col[:, None] * inv], axis=-1)
    cos = jnp.repeat(jnp.cos(ang), 2, axis=-1)
    sin = jnp.stack([-jnp.sin(ang), jnp.sin(ang)], axis=-1).reshape(n_lat, C_DH)
    pad = n_rows - n_lat
    cos = jnp.concatenate([cos, jnp.ones((pad, C_DH), F32)], axis=0)
    sin = jnp.concatenate([sin, jnp.zeros((pad, C_DH), F32)], axis=0)
    return cos, sin


def kernel(x, c, ctx, c_ctx, ada_down, ada_up, ada_bias, norm_g, ffn_w13, ffn_w2, w_in, gmlp_norm_g,
           gmlp_ws, gmlp_bs, mlstm_gate_b, mlstm_norm_g, diff_lambda, diff_norm_g, w_branch, w_out,
           final_g):
    bsz, seq, d = x.shape
    n_ctx = ctx.shape[1]
    depth = ada_down.shape[0]
    n_tok = seq + n_ctx
    assert bsz == 1 and seq % ELEM_ROWS == 0 and n_ctx % ELEM_ROWS == 0
    assert w_in.shape[-1] == Z_GATE + GATE_COLS + N_BRANCH * d

    xs = jnp.concatenate([x[0], ctx[0]], axis=0)
    w13 = ffn_w13.astype(BF16)
    w2 = ffn_w2.astype(BF16)
    w_main, w_gate = _pack_w_in(w_in)
    wbr = w_branch.astype(BF16)
    wo = w_out.astype(BF16)
    cos_t, sin_t = _rope_tables(seq, n_tok)
    gate_b = jnp.pad(mlstm_gate_b.reshape(depth, 1, GATE_COLS), ((0, 0), (0, 0), (0, LANES - GATE_COLS)))
    cv = jnp.concatenate([c, c_ctx[None], jnp.zeros((6, d), F32)], axis=0)

    mod = _modulation(cv, ada_down, ada_up, ada_bias)[:, :2].reshape(depth, 2, N_SUB, 3, d)

    tq = 1024 if seq % 1024 == 0 else ELEM_ROWS
    tk = 1408 if n_tok % 1408 == 0 else ELEM_ROWS
    for l in range(depth):
        lam_init = 0.8 - 0.6 * math.exp(-0.3 * l)

        def ffn(xs, s, sub, tm, x_buffers):
            h = _normmod(xs, norm_g[l, sub:sub + 1], mod[l, :, sub], seq, tm, x_buffers)
            act = _ffn_up(h, w13, l, s)
            return _resid_proj(act, w2, (l, s), xs, mod[l, :, sub, 2:3], seq, 0.5)

        xs = ffn(xs, 0, 0, ELEM_ROWS, 2)

        h = _normmod(xs, norm_g[l, 1:2], mod[l, :, 1], seq, _row_tile(n_tok, 128), 2)
        z, zg = _proj(h, w_main, w_gate, gate_b[l], l)
        ya = _gmlp(z, n_tok, gmlp_norm_g[l][None], gmlp_ws[l], gmlp_bs[l].T)
        hf, hb = _mlstm(z, zg, seq // CHUNK, n_ctx // CHUNK)
        yb = _mlstm_out(hf, hb, z, mlstm_norm_g[l][None])
        qr, kr = _rope(z, cos_t, sin_t)
        ng = diff_norm_g[l][None]
        yc_lat = _attention(qr, kr, z, diff_lambda[l], ng, lam_init, 0, seq, 0, n_tok, tq, tk)
        yc_ctx = _attention(qr, kr, z, diff_lambda[l], ng, lam_init, seq, n_ctx, seq, n_ctx, n_ctx, n_ctx)
        yc = jnp.concatenate([yc_lat, yc_ctx], axis=0)
        y = _merge(ya, yb, yc, z, wbr, l)
        xs = _resid_proj(y, wo, (l,), xs, mod[l, :, 1, 2:3], seq, 1.0)

        xs = ffn(xs, 1, 2, _row_tile(n_tok, 384), 2)
    return _final_norm(xs, final_g[None], seq)[None]
```

```python
import functools
import math

import jax
import jax.numpy as jnp
from jax import lax
from jax.experimental import pallas as pl
from jax.experimental.pallas import tpu as pltpu

F32 = jnp.float32
BF16 = jnp.bfloat16

GRID_W = 64
CHUNK = 128
BRANCH_WIDTH = 1024
A_GROUPS = 8
A_GDIM = BRANCH_WIDTH // A_GROUPS
B_HEADS = 4
B_DQK = 128
B_DV = BRANCH_WIDTH // B_HEADS
C_HEADS = 4
C_DH = 128
C_DV = 2 * C_DH
N_BRANCH = 3
N_SUB = 3
ROPE_THETA = 10000.0
EPS = 1e-6
GATE_COLS = 4 * B_HEADS
LANES = 128
Z_UV, Z_MQ, Z_MK, Z_MV, Z_MO, Z_AQ, Z_AK, Z_AV, Z_GATE = (
    0, 2048, 2560, 3072, 4096, 5120, 6144, 7168, 8192)
DENSE_ROWS = 1408
ROW_SPLIT = 2
ELEM_ROWS = 256
NORM_ROWS = 384
MLSTM_STAGES = 5
ATTN_STAGES = 5
ATTN_GROUP = 2
ATTN_ROW_SPLIT = 4
VMEM_LIMIT = 56 * 1024 * 1024
LOG2E = math.log2(math.e)


def _params(sem):
    return pltpu.CompilerParams(dimension_semantics=sem, vmem_limit_bytes=VMEM_LIMIT)


def _dot(a, b):
    return jnp.dot(a, b, preferred_element_type=F32)


def _dot_nt(a, b):
    return lax.dot_general(a, b, (((1,), (1,)), ((), ())), preferred_element_type=F32)


def _sigmoid(x):
    return 0.5 * jnp.tanh(0.5 * x) + 0.5


def _log_sigmoid(x):
    return jnp.minimum(x, 0.0) - jnp.log(1.0 + jnp.exp(-jnp.abs(x)))


def _row_tile(n, cap):
    unit = 16 * ROW_SPLIT
    return max(t for t in range(unit, min(n, cap) + 1, unit) if n % t == 0)


def _for_row_subtiles(tm, body):
    sub = tm // ROW_SPLIT

    def step(r, carry):
        body(pl.ds(pl.multiple_of(r * sub, sub), sub), r * sub)
        return carry

    lax.fori_loop(0, ROW_SPLIT, step, 0)


def _is_ctx(row0, n, seq):
    return row0 + lax.broadcasted_iota(jnp.int32, (n, 1), 0) >= seq


def _mod_kernel(cv_ref, down_ref, up_ref, b_ref, o_ref, t_ref):
    @pl.when(pl.program_id(1) == 0)
    def _():
        s = cv_ref[...]
        s = s * _sigmoid(s)
        t_ref[...] = _dot(s.astype(BF16), down_ref[...].astype(BF16))

    o_ref[...] = _dot(t_ref[...].astype(BF16), up_ref[...].astype(BF16)) + b_ref[...]


def _modulation(cv, ada_down, ada_up, ada_bias):
    depth, d, r = ada_down.shape
    n = ada_up.shape[-1]
    tn = 3 * d
    return pl.pallas_call(
        _mod_kernel,
        grid=(depth, n // tn),
        in_specs=[
            pl.BlockSpec((8, d), lambda l, j: (0, 0)),
            pl.BlockSpec((None, d, r), lambda l, j: (l, 0, 0)),
            pl.BlockSpec((None, r, tn), lambda l, j: (l, 0, j)),
            pl.BlockSpec((None, 1, tn), lambda l, j: (l, 0, j)),
        ],
        out_specs=pl.BlockSpec((None, 8, tn), lambda l, j: (l, 0, j)),
        out_shape=jax.ShapeDtypeStruct((depth, 8, n), F32),
        scratch_shapes=[pltpu.VMEM((8, r), F32)],
        compiler_params=_params(("parallel", "arbitrary")),
        name="adaln_modulation",
    )(cv, ada_down, ada_up, ada_bias.reshape(depth, 1, n))


def _normmod_kernel(x_ref, g_ref, mod_ref, h_ref, *, seq, tm):
    x = x_ref[...]
    ctx = _is_ctx(pl.program_id(0) * tm, tm, seq)
    shift = jnp.where(ctx, mod_ref[1, 0:1, :], mod_ref[0, 0:1, :])
    scale = jnp.where(ctx, mod_ref[1, 1:2, :], mod_ref[0, 1:2, :])
    y = x * lax.rsqrt(jnp.mean(x * x, axis=-1, keepdims=True) + EPS)
    h_ref[...] = ((y * g_ref[...]) * (1.0 + scale) + shift).astype(h_ref.dtype)


def _normmod(xs, g, mod, seq):
    n, d = xs.shape
    tm = _row_tile(n, NORM_ROWS)
    return pl.pallas_call(
        functools.partial(_normmod_kernel, seq=seq, tm=tm),
        grid=(n // tm,),
        in_specs=[
            pl.BlockSpec((tm, d), lambda i: (i, 0)),
            pl.BlockSpec((1, d), lambda i: (0, 0)),
            pl.BlockSpec((2, 3, d), lambda i: (0, 0, 0)),
        ],
        out_specs=pl.BlockSpec((tm, d), lambda i: (i, 0)),
        out_shape=jax.ShapeDtypeStruct((n, d), BF16),
        compiler_params=_params(("parallel",)),
        name="norm_modulate",
    )(xs, g, mod)


def _ffn_up_kernel(h_ref, wa_ref, wb_ref, o_ref, *, tm):
    def body(rows, _):
        h = h_ref[rows, :]
        a = _dot(h, wa_ref[...])
        b = _dot(h, wb_ref[...])
        o_ref[rows, :] = (a * _sigmoid(a) * b).astype(o_ref.dtype)

    _for_row_subtiles(tm, body)


def _ffn_up(h, w13, l, s):
    n, d = h.shape
    f = w13.shape[-1] // 2
    tm = _row_tile(n, DENSE_ROWS)
    tf = min(f, 512)
    nj = f // tf
    return pl.pallas_call(
        functools.partial(_ffn_up_kernel, tm=tm),
        grid=(n // tm, nj),
        in_specs=[
            pl.BlockSpec((tm, d), lambda i, j: (i, 0)),
            pl.BlockSpec((None, None, d, tf), lambda i, j: (l, s, 0, j)),
            pl.BlockSpec((None, None, d, tf), lambda i, j: (l, s, 0, nj + j)),
        ],
        out_specs=pl.BlockSpec((tm, tf), lambda i, j: (i, j)),
        out_shape=jax.ShapeDtypeStruct((n, f), BF16),
        compiler_params=_params(("parallel", "arbitrary")),
        name="ffn_up",
    )(h, w13, w13)


def _resid_kernel(a_ref, w_ref, x_ref, gate_ref, o_ref, *, tm, seq, coef):
    row_base = pl.program_id(0) * tm

    def body(rows, r0):
        sub = tm // ROW_SPLIT
        gate = jnp.where(_is_ctx(row_base + r0, sub, seq), gate_ref[1], gate_ref[0])
        o_ref[rows, :] = x_ref[rows, :] + (coef * gate) * _dot(a_ref[rows, :], w_ref[...])

    _for_row_subtiles(tm, body)


def _resid_proj(a, w, widx, xs, gate, seq, coef):
    n, d = xs.shape
    k = a.shape[1]
    tm = _row_tile(n, DENSE_ROWS)
    tn = min(d, 512)
    lead = (None,) * len(widx)
    return pl.pallas_call(
        functools.partial(_resid_kernel, tm=tm, seq=seq, coef=coef),
        grid=(n // tm, d // tn),
        in_specs=[
            pl.BlockSpec((tm, k), lambda i, j: (i, 0)),
            pl.BlockSpec(lead + (k, tn), lambda i, j: widx + (0, j)),
            pl.BlockSpec((tm, tn), lambda i, j: (i, j)),
            pl.BlockSpec((2, 1, tn), lambda i, j: (0, 0, j)),
        ],
        out_specs=pl.BlockSpec((tm, tn), lambda i, j: (i, j)),
        out_shape=jax.ShapeDtypeStruct((n, d), F32),
        compiler_params=_params(("parallel", "arbitrary")),
        name="gated_residual_proj",
    )(a, w, xs, gate)


def _proj_kernel(h_ref, wt_ref, wgt_ref, gb_ref, z_ref, zg_ref, *, tm):
    j = pl.program_id(1)

    def body(rows, _):
        h = h_ref[rows, :]
        z_ref[rows, :] = _dot_nt(h, wt_ref[0]).astype(z_ref.dtype)

        @pl.when(j == 0)
        def _():
            zg_ref[rows, :] = _dot_nt(h, wgt_ref[0]) + gb_ref[...]

    _for_row_subtiles(tm, body)


def _proj(h, w_t, gate_b, l):
    n, d = h.shape
    g0 = Z_MO + BRANCH_WIDTH
    nz = w_t.shape[1] - GATE_COLS
    tm = _row_tile(n, DENSE_ROWS)
    tn = 1024
    assert g0 % tn == 0 and nz % tn == 0
    return pl.pallas_call(
        functools.partial(_proj_kernel, tm=tm),
        grid=(n // tm, nz // tn),
        in_specs=[
            pl.BlockSpec((tm, d), lambda i, j: (i, 0)),
            pl.BlockSpec((pl.Element(1), pl.Element(tn), pl.Element(d)),
                         lambda i, j: (l, pl.multiple_of(
                             j * tn + jnp.where(j * tn >= g0, GATE_COLS, 0), GATE_COLS), 0)),
            pl.BlockSpec((pl.Element(1), pl.Element(LANES), pl.Element(d)), lambda i, j: (l, g0, 0)),
            pl.BlockSpec((1, LANES), lambda i, j: (0, 0)),
        ],
        out_specs=[
            pl.BlockSpec((tm, tn), lambda i, j: (i, j)),
            pl.BlockSpec((tm, LANES), lambda i, j: (i, 0)),
        ],
        out_shape=[
            jax.ShapeDtypeStruct((n, nz), BF16),
            jax.ShapeDtypeStruct((n, LANES), F32),
        ],
        compiler_params=_params(("parallel", "arbitrary")),
        name="mixer_in_proj",
    )(h, w_t, w_t, gate_b)


def _gmlp_kernel(u_ref, v_ref, ng_ref, ws_ref, bst_ref, o_ref, *, tm):
    v = jax.nn.gelu(v_ref[...].astype(F32))
    vc = v - jnp.mean(v, axis=-1, keepdims=True)
    vn = vc * lax.rsqrt(jnp.mean(vc * vc, axis=-1, keepdims=True) + EPS) * ng_ref[...]
    vn = vn.astype(BF16)
    for c in range(tm // CHUNK):
        rows = slice(c * CHUNK, (c + 1) * CHUNK)
        for g in range(A_GROUPS):
            cols = slice(g * A_GDIM, (g + 1) * A_GDIM)
            mixed = _dot(ws_ref[g].astype(BF16), vn[rows, cols]) + bst_ref[:, g:g + 1]
            u = jax.nn.gelu(u_ref[rows, cols].astype(F32))
            o_ref[rows, cols] = (u * mixed).astype(o_ref.dtype)


def _gmlp(z, n, ng, ws, bst):
    tm = ELEM_ROWS
    bw = BRANCH_WIDTH
    return pl.pallas_call(
        functools.partial(_gmlp_kernel, tm=tm),
        grid=(n // tm,),
        in_specs=[
            pl.BlockSpec((tm, bw), lambda i: (i, Z_UV // bw)),
            pl.BlockSpec((tm, bw), lambda i: (i, Z_UV // bw + 1)),
            pl.BlockSpec((1, bw), lambda i: (0, 0)),
            pl.BlockSpec((A_GROUPS, CHUNK, CHUNK), lambda i: (0, 0, 0)),
            pl.BlockSpec((CHUNK, A_GROUPS), lambda i: (0, 0)),
        ],
        out_specs=pl.BlockSpec((tm, bw), lambda i: (i, 0)),
        out_shape=jax.ShapeDtypeStruct((n, bw), BF16),
        compiler_params=_params(("parallel",)),
        name="gmlp_branch",
    )(z, z, ng, ws, bst)


def _mlstm_head(rev, q, k, v, gi_col, gf_col, gi_row, gf_row, c_state, n_state, m_state):
    L = q.shape[0]
    r = lax.broadcasted_iota(jnp.int32, (L, L), 0)
    c = lax.broadcasted_iota(jnp.int32, (L, L), 1)
    seen = (c >= r) if rev else (c <= r)
    seen_t = (r >= c) if rev else (r <= c)
    lf_col = _log_sigmoid(gf_col)
    lf_row = _log_sigmoid(gf_row)
    b_col = jnp.sum(jnp.where(seen, lf_row, 0.0), axis=1, keepdims=True)
    b_row = jnp.sum(jnp.where(seen_t, lf_col, 0.0), axis=0, keepdims=True)
    b_last = jnp.sum(lf_row, axis=1, keepdims=True)
    qk = _dot_nt(q, k)
    yield

    dmat = jnp.where(seen, b_col - b_row + gi_row, -jnp.inf)
    a_col = b_col + m_state
    m_t = jnp.maximum(a_col, jnp.max(dmat, axis=1, keepdims=True))
    g_row = b_last - b_row + gi_row
    g_col = b_last - b_col + gi_col
    m_new = jnp.maximum(b_last + m_state, jnp.max(g_row, axis=1, keepdims=True))
    yield

    wa = jnp.exp(a_col - m_t)
    s = qk * jnp.exp(dmat - m_t)
    decay = jnp.exp(b_last + m_state - m_new)
    kw = k.astype(F32) * jnp.exp(g_col - m_new)
    yield

    sv = _dot(s.astype(BF16), v)
    qc = _dot(q, c_state.astype(BF16))
    kv = _dot(kw.T.astype(BF16), v)
    yield

    qn = jnp.sum(q.astype(F32) * n_state, axis=1, keepdims=True)
    den = wa * qn + jnp.sum(s, axis=1, keepdims=True)
    h = (wa * qc + sv) / jnp.maximum(jnp.abs(den), jnp.exp(-m_t))
    c_new = decay * c_state + kv
    n_new = decay * n_state + jnp.sum(kw, axis=0, keepdims=True)
    yield h, c_new, n_new, m_new


def _mlstm_kernel(qf_ref, kf_ref, vf_ref, gf_ref, qb_ref, kb_ref, vb_ref, gb_ref,
                  hf_ref, hb_ref, c_ref, n_ref, m_ref):
    @pl.when(pl.program_id(0) == 0)
    def _():
        c_ref[...] = jnp.zeros_like(c_ref)
        n_ref[...] = jnp.zeros_like(n_ref)
        m_ref[...] = jnp.zeros_like(m_ref)

    heads = []
    for d, (q_ref, k_ref, v_ref, g_ref, h_ref) in enumerate(
            ((qf_ref, kf_ref, vf_ref, gf_ref, hf_ref), (qb_ref, kb_ref, vb_ref, gb_ref, hb_ref))):
        gates = g_ref[...]
        gates_t = gates.T
        for hd in range(B_HEADS):
            idx = d * B_HEADS + hd
            ci = d * 2 * B_HEADS + hd
            cf = ci + B_HEADS
            qs = slice(hd * B_DQK, (hd + 1) * B_DQK)
            vs = slice(hd * B_DV, (hd + 1) * B_DV)
            q = (q_ref[:, qs].astype(F32) * (B_DQK ** -0.5)).astype(BF16)
            stages = _mlstm_head(
                d == 1, q, k_ref[:, qs], v_ref[:, vs],
                gates[:, ci:ci + 1], gates[:, cf:cf + 1],
                gates_t[ci:ci + 1, :], gates_t[cf:cf + 1, :],
                c_ref[idx], n_ref[idx], m_ref[idx][:, 0:1])
            heads.append((stages, idx, h_ref, vs))

    for _ in range(MLSTM_STAGES - 1):
        for stages, _, _, _ in heads:
            next(stages)
    for stages, idx, h_ref, vs in heads:
        h, c_new, n_new, m_new = next(stages)
        h_ref[:, vs] = h
        c_ref[idx] = c_new
        n_ref[idx] = n_new
        m_ref[idx] = jnp.broadcast_to(m_new, (1, LANES))


def _mlstm(z, zg, n_lat, n_ctx):
    n_steps = n_lat + n_ctx
    n = n_steps * CHUNK

    def fwd(t):
        return (t + n_lat) % n_steps

    def bwd(t):
        return n_steps - 1 - t

    def specs(order):
        return [
            pl.BlockSpec((CHUNK, B_HEADS * B_DQK), lambda t: (order(t), Z_MQ // (B_HEADS * B_DQK))),
            pl.BlockSpec((CHUNK, B_HEADS * B_DQK), lambda t: (order(t), Z_MK // (B_HEADS * B_DQK))),
            pl.BlockSpec((CHUNK, BRANCH_WIDTH), lambda t: (order(t), Z_MV // BRANCH_WIDTH)),
            pl.BlockSpec((CHUNK, LANES), lambda t: (order(t), 0)),
        ]

    return pl.pallas_call(
        _mlstm_kernel,
        grid=(n_steps,),
        in_specs=specs(fwd) + specs(bwd),
        out_specs=[
            pl.BlockSpec((CHUNK, BRANCH_WIDTH), lambda t: (fwd(t), 0)),
            pl.BlockSpec((CHUNK, BRANCH_WIDTH), lambda t: (bwd(t), 0)),
        ],
        out_shape=[jax.ShapeDtypeStruct((n, BRANCH_WIDTH), F32)] * 2,
        scratch_shapes=[
            pltpu.VMEM((2 * B_HEADS, B_DQK, B_DV), F32),
            pltpu.VMEM((2 * B_HEADS, 1, B_DQK), F32),
            pltpu.VMEM((2 * B_HEADS, 1, LANES), F32),
        ],
        compiler_params=_params(("arbitrary",)),
        name="mlstm_bidir",
    )(z, z, z, zg, z, z, z, zg)


def _mlstm_out_kernel(hf_ref, hb_ref, og_ref, g_ref, o_ref):
    for hd in range(B_HEADS):
        cols = slice(hd * B_DV, (hd + 1) * B_DV)
        h = hf_ref[:, cols] + hb_ref[:, cols]
        hn = h * lax.rsqrt(jnp.mean(h * h, axis=-1, keepdims=True) + EPS) * g_ref[:, cols]
        o_ref[:, cols] = (_sigmoid(og_ref[:, cols].astype(F32)) * hn).astype(o_ref.dtype)


def _mlstm_out(hf, hb, z, g):
    n = hf.shape[0]
    tm = ELEM_ROWS
    bw = BRANCH_WIDTH
    row = lambda i: (i, 0)
    return pl.pallas_call(
        _mlstm_out_kernel,
        grid=(n // tm,),
        in_specs=[
            pl.BlockSpec((tm, bw), row),
            pl.BlockSpec((tm, bw), row),
            pl.BlockSpec((tm, bw), lambda i: (i, Z_MO // bw)),
            pl.BlockSpec((1, bw), lambda i: (0, 0)),
        ],
        out_specs=pl.BlockSpec((tm, bw), row),
        out_shape=jax.ShapeDtypeStruct((n, bw), BF16),
        compiler_params=_params(("parallel",)),
        name="mlstm_out_gate",
    )(hf, hb, z, g)


def _rope_kernel(q_ref, k_ref, cos_ref, sin_ref, qo_ref, ko_ref):
    cos = cos_ref[...]
    sin = sin_ref[...]
    even = lax.broadcasted_iota(jnp.int32, cos.shape, 1) % 2 == 0
    for src, dst, mult in ((q_ref, qo_ref, C_DH ** -0.5 * LOG2E), (k_ref, ko_ref, None)):
        for g in range(2 * C_HEADS):
            cols = slice(g * C_DH, (g + 1) * C_DH)
            x = src[:, cols].astype(F32)
            partner = jnp.where(even, pltpu.roll(x, C_DH - 1, 1), pltpu.roll(x, 1, 1))
            y = x * cos + partner * sin
            if mult is not None:
                y = y * mult
            dst[:, cols] = y.astype(dst.dtype)


def _rope(z, cos_t, sin_t):
    n = cos_t.shape[0]
    tm = ELEM_ROWS
    w = 2 * C_HEADS * C_DH
    return pl.pallas_call(
        _rope_kernel,
        grid=(n // tm,),
        in_specs=[
            pl.BlockSpec((tm, w), lambda i: (i, Z_AQ // w)),
            pl.BlockSpec((tm, w), lambda i: (i, Z_AK // w)),
            pl.BlockSpec((tm, C_DH), lambda i: (i, 0)),
            pl.BlockSpec((tm, C_DH), lambda i: (i, 0)),
        ],
        out_specs=[pl.BlockSpec((tm, w), lambda i: (i, 0))] * 2,
        out_shape=[jax.ShapeDtypeStruct((n, w), BF16)] * 2,
        compiler_params=_params(("parallel",)),
        name="axial_rope",
    )(z, z, cos_t, sin_t)


def _attn_kernel(q_ref, k_ref, v_ref, lam_ref, ng_ref, o_ref,
                 m1_ref, l1_ref, a1_ref, m2_ref, l2_ref, a2_ref, *, nk, tk, lam_init):
    stats = ((m1_ref, l1_ref, a1_ref), (m2_ref, l2_ref, a2_ref))
    for m_ref, l_ref, a_ref in stats:
        m_ref[...] = jnp.full_like(m_ref, -jnp.inf)
        l_ref[...] = jnp.zeros_like(l_ref)
        a_ref[...] = jnp.zeros_like(a_ref)

    sub = q_ref.shape[0] // ATTN_ROW_SPLIT

    def block(keys, r, mp_i, m_ref, l_ref, a_ref):
        rows = slice(r * sub, (r + 1) * sub)
        cols = slice(mp_i * C_DH, (mp_i + 1) * C_DH)
        s = _dot_nt(q_ref[rows, cols], k_ref[keys, cols])
        yield
        m_prev = m_ref[rows, :]
        m_new = jnp.maximum(m_prev, jnp.max(s, axis=-1, keepdims=True))
        alpha = jnp.exp2(m_prev - m_new)
        m_ref[rows, :] = m_new
        yield
        p = jnp.exp2(s - m_new)
        l_ref[rows, :] = alpha * l_ref[rows, :] + jnp.sum(p, axis=-1, keepdims=True)
        yield
        pv = _dot(p.astype(BF16), v_ref[keys, :])
        yield
        a_ref[rows, :] = alpha * a_ref[rows, :] + pv
        yield

    def key_chunk(c, carry):
        keys = pl.ds(pl.multiple_of(c * tk, tk), tk)
        for r0 in range(0, ATTN_ROW_SPLIT, ATTN_GROUP):
            blocks = [block(keys, r, mp_i, *refs)
                      for r in range(r0, r0 + ATTN_GROUP) for mp_i, refs in enumerate(stats)]
            for _ in range(ATTN_STAGES):
                for b in blocks:
                    next(b)
        return carry

    lax.fori_loop(0, nk, key_chunk, 0)

    lp = lam_ref[...]
    lam = (jnp.exp(jnp.sum(lp[0:1] * lp[1:2], axis=-1, keepdims=True))
           - jnp.exp(jnp.sum(lp[2:3] * lp[3:4], axis=-1, keepdims=True)) + lam_init)
    o = a1_ref[...] / l1_ref[...] - lam * (a2_ref[...] / l2_ref[...])
    y = o * lax.rsqrt(jnp.mean(o * o, axis=-1, keepdims=True) + EPS) * ng_ref[...]
    o_ref[...] = (y * (1.0 - lam_init)).astype(o_ref.dtype)


def _attention(qr, kr, z, lam_p, ng, lam_init, q_row0, n_q_rows, kv_row0, n_kv_rows, tq, tk):
    nq = n_q_rows // tq
    nk = n_kv_rows // tk
    qb0 = q_row0 // tq
    kvb = kv_row0 // n_kv_rows
    assert kv_row0 % n_kv_rows == 0
    return pl.pallas_call(
        functools.partial(_attn_kernel, nk=nk, tk=tk, lam_init=lam_init),
        grid=(C_HEADS, nq),
        in_specs=[
            pl.BlockSpec((tq, C_DV), lambda h, i: (qb0 + i, h)),
            pl.BlockSpec((n_kv_rows, C_DV), lambda h, i: (kvb, h)),
            pl.BlockSpec((n_kv_rows, C_DV), lambda h, i: (kvb, Z_AV // C_DV + h)),
            pl.BlockSpec((4, C_DH), lambda h, i: (0, 0)),
            pl.BlockSpec((1, C_DV), lambda h, i: (0, 0)),
        ],
        out_specs=pl.BlockSpec((tq, C_DV), lambda h, i: (i, h)),
        out_shape=jax.ShapeDtypeStruct((n_q_rows, C_HEADS * C_DV), BF16),
        scratch_shapes=[
            pltpu.VMEM((tq, 1), F32), pltpu.VMEM((tq, 1), F32), pltpu.VMEM((tq, C_DV), F32),
            pltpu.VMEM((tq, 1), F32), pltpu.VMEM((tq, 1), F32), pltpu.VMEM((tq, C_DV), F32),
        ],
        compiler_params=_params(("parallel", "arbitrary")),
        name="diff_attention",
    )(qr, kr, z, lam_p, ng)


def _merge_kernel(ya_ref, yb_ref, yc_ref, g0_ref, g1_ref, g2_ref, wb_ref, y_ref, *, tm):
    def body(rows, _):
        y = _sigmoid(g0_ref[rows, :].astype(F32)) * _dot(ya_ref[rows, :], wb_ref[0])
        y += _sigmoid(g1_ref[rows, :].astype(F32)) * _dot(yb_ref[rows, :], wb_ref[1])
        y += _sigmoid(g2_ref[rows, :].astype(F32)) * _dot(yc_ref[rows, :], wb_ref[2])
        y_ref[rows, :] = y.astype(y_ref.dtype)

    _for_row_subtiles(tm, body)


def _merge(ya, yb, yc, z, w_branch, l):
    n = ya.shape[0]
    d = w_branch.shape[-1]
    tm = _row_tile(n, DENSE_ROWS)
    tn = min(d, 512)
    bw = BRANCH_WIDTH
    gate_blk = Z_GATE // tn
    row = lambda i, j: (i, 0)
    return pl.pallas_call(
        functools.partial(_merge_kernel, tm=tm),
        grid=(n // tm, d // tn),
        in_specs=[
            pl.BlockSpec((tm, bw), row),
            pl.BlockSpec((tm, bw), row),
            pl.BlockSpec((tm, bw), row),
            pl.BlockSpec((tm, tn), lambda i, j: (i, gate_blk + j)),
            pl.BlockSpec((tm, tn), lambda i, j: (i, gate_blk + d // tn + j)),
            pl.BlockSpec((tm, tn), lambda i, j: (i, gate_blk + 2 * (d // tn) + j)),
            pl.BlockSpec((None, N_BRANCH, bw, tn), lambda i, j: (l, 0, 0, j)),
        ],
        out_specs=pl.BlockSpec((tm, tn), lambda i, j: (i, j)),
        out_shape=jax.ShapeDtypeStruct((n, d), BF16),
        compiler_params=_params(("parallel", "arbitrary")),
        name="branch_merge",
    )(ya, yb, yc, z, z, z, w_branch)


def _final_kernel(x_ref, g_ref, o_ref):
    x = x_ref[...]
    o_ref[...] = x * lax.rsqrt(jnp.mean(x * x, axis=-1, keepdims=True) + EPS) * g_ref[...]


def _final_norm(xs, g, n_rows):
    d = xs.shape[1]
    tm = ELEM_ROWS
    return pl.pallas_call(
        _final_kernel,
        grid=(n_rows // tm,),
        in_specs=[pl.BlockSpec((tm, d), lambda i: (i, 0)), pl.BlockSpec((1, d), lambda i: (0, 0))],
        out_specs=pl.BlockSpec((tm, d), lambda i: (i, 0)),
        out_shape=jax.ShapeDtypeStruct((n_rows, d), F32),
        compiler_params=_params(("parallel",)),
        name="final_norm",
    )(xs, g)


def _rope_tables(n_lat, n_rows):
    t = jnp.arange(n_lat)
    r = (t // GRID_W).astype(F32)
    col = (t % GRID_W).astype(F32)
    n_freq = C_DH // 4
    inv = ROPE_THETA ** (-jnp.arange(n_freq, dtype=F32) / n_freq)
    ang = jnp.concatenate([r[:, None] * inv, col[:, None] * inv], axis=-1)
    cos = jnp.repeat(jnp.cos(ang), 2, axis=-1)
    sin = jnp.stack([-jnp.sin(ang), jnp.sin(ang)], axis=-1).reshape(n_lat, C_DH)
    pad = n_rows - n_lat
    cos = jnp.concatenate([cos, jnp.ones((pad, C_DH), F32)], axis=0)
    sin = jnp.concatenate([sin, jnp.zeros((pad, C_DH), F32)], axis=0)
    return cos, sin


def kernel(x, c, ctx, c_ctx, ada_down, ada_up, ada_bias, norm_g, ffn_w13, ffn_w2, w_in, gmlp_norm_g,
           gmlp_ws, gmlp_bs, mlstm_gate_b, mlstm_norm_g, diff_lambda, diff_norm_g, w_branch, w_out,
           final_g):
    bsz, seq, d = x.shape
    n_ctx = ctx.shape[1]
    depth = ada_down.shape[0]
    n_tok = seq + n_ctx
    assert bsz == 1 and seq % ELEM_ROWS == 0 and n_ctx % ELEM_ROWS == 0
    assert w_in.shape[-1] == Z_GATE + GATE_COLS + N_BRANCH * d

    xs = jnp.concatenate([x[0], ctx[0]], axis=0)
    w13 = ffn_w13.astype(BF16)
    w2 = ffn_w2.astype(BF16)
    w_in_t = jnp.swapaxes(w_in, 1, 2).astype(BF16)
    wbr = w_branch.astype(BF16)
    wo = w_out.astype(BF16)
    cos_t, sin_t = _rope_tables(seq, n_tok)
    gate_b = jnp.pad(mlstm_gate_b.reshape(depth, 1, GATE_COLS), ((0, 0), (0, 0), (0, LANES - GATE_COLS)))
    cv = jnp.concatenate([c, c_ctx[None], jnp.zeros((6, d), F32)], axis=0)

    mod = _modulation(cv, ada_down, ada_up, ada_bias)[:, :2].reshape(depth, 2, N_SUB, 3, d)

    tq = 1024 if seq % 1024 == 0 else ELEM_ROWS
    tk = 1408 if n_tok % 1408 == 0 else ELEM_ROWS
    for l in range(depth):
        lam_init = 0.8 - 0.6 * math.exp(-0.3 * l)

        def ffn(xs, s, sub):
            h = _normmod(xs, norm_g[l, sub:sub + 1], mod[l, :, sub], seq)
            act = _ffn_up(h, w13, l, s)
            return _resid_proj(act, w2, (l, s), xs, mod[l, :, sub, 2:3], seq, 0.5)

        xs = ffn(xs, 0, 0)

        h = _normmod(xs, norm_g[l, 1:2], mod[l, :, 1], seq)
        z, zg = _proj(h, w_in_t, gate_b[l], l)
        ya = _gmlp(z, n_tok, gmlp_norm_g[l][None], gmlp_ws[l], gmlp_bs[l].T)
        hf, hb = _mlstm(z, zg, seq // CHUNK, n_ctx // CHUNK)
        yb = _mlstm_out(hf, hb, z, mlstm_norm_g[l][None])
        qr, kr = _rope(z, cos_t, sin_t)
        ng = diff_norm_g[l][None]
        yc_lat = _attention(qr, kr, z, diff_lambda[l], ng, lam_init, 0, seq, 0, n_tok, tq, tk)
        yc_ctx = _attention(qr, kr, z, diff_lambda[l], ng, lam_init, seq, n_ctx, seq, n_ctx, n_ctx, n_ctx)
        yc = jnp.concatenate([yc_lat, yc_ctx], axis=0)
        y = _merge(ya, yb, yc, z, wbr, l)
        xs = _resid_proj(y, wo, (l,), xs, mod[l, :, 1, 2:3], seq, 1.0)

        xs = ffn(xs, 1, 2)
    return _final_norm(xs, final_g[None], seq)[None]
```

```python
import functools
import math

import jax
import jax.numpy as jnp
from jax import lax
from jax.experimental import pallas as pl
from jax.experimental.pallas import tpu as pltpu

F32 = jnp.float32
BF16 = jnp.bfloat16

GRID_W = 64
CHUNK = 128
BRANCH_WIDTH = 1024
A_GROUPS = 8
A_GDIM = BRANCH_WIDTH // A_GROUPS
B_HEADS = 4
B_DQK = 128
B_DV = BRANCH_WIDTH // B_HEADS
C_HEADS = 4
C_DH = 128
C_DV = 2 * C_DH
N_BRANCH = 3
N_SUB = 3
ROPE_THETA = 10000.0
EPS = 1e-6
GATE_COLS = 4 * B_HEADS
LANES = 128
Z_UV, Z_MQ, Z_MK, Z_MV, Z_MO, Z_AQ, Z_AK, Z_AV, Z_GATE = (
    0, 2048, 2560, 3072, 4096, 5120, 6144, 7168, 8192)
DENSE_ROWS = 1408
ROW_SPLIT = 2
HEAD_COLS = 256
ELEM_ROWS = 256
NORM_ROWS = 384
MLSTM_STAGES = 5
ATTN_STAGES = 5
ATTN_KEY_UNROLL = 3
ATTN_GROUP = 2
ATTN_ROW_SPLIT = 4
VMEM_LIMIT = 56 * 1024 * 1024
LOG2E = math.log2(math.e)


def _params(sem):
    return pltpu.CompilerParams(dimension_semantics=sem, vmem_limit_bytes=VMEM_LIMIT)


def _dot(a, b):
    return jnp.dot(a, b, preferred_element_type=F32)


def _dot_nt(a, b):
    return lax.dot_general(a, b, (((1,), (1,)), ((), ())), preferred_element_type=F32)


def _sigmoid(x):
    return 0.5 * jnp.tanh(0.5 * x) + 0.5


def _log_sigmoid(x):
    return jnp.minimum(x, 0.0) - jnp.log(1.0 + jnp.exp(-jnp.abs(x)))


def _row_tile(n, cap):
    unit = 16 * ROW_SPLIT
    return max(t for t in range(unit, min(n, cap) + 1, unit) if n % t == 0)


def _for_row_subtiles(tm, body):
    sub = tm // ROW_SPLIT

    def step(r, carry):
        body(pl.ds(pl.multiple_of(r * sub, sub), sub), r * sub)
        return carry

    lax.fori_loop(0, ROW_SPLIT, step, 0)


def _is_ctx(row0, n, seq):
    return row0 + lax.broadcasted_iota(jnp.int32, (n, 1), 0) >= seq


def _mod_kernel(cv_ref, down_ref, up_ref, b_ref, o_ref, t_ref):
    @pl.when(pl.program_id(1) == 0)
    def _():
        s = cv_ref[...]
        s = s * _sigmoid(s)
        t_ref[...] = _dot(s.astype(BF16), down_ref[...].astype(BF16))

    o_ref[...] = _dot(t_ref[...].astype(BF16), up_ref[...].astype(BF16)) + b_ref[...]


def _modulation(cv, ada_down, ada_up, ada_bias):
    depth, d, r = ada_down.shape
    n = ada_up.shape[-1]
    tn = 3 * d
    return pl.pallas_call(
        _mod_kernel,
        grid=(depth, n // tn),
        in_specs=[
            pl.BlockSpec((8, d), lambda l, j: (0, 0)),
            pl.BlockSpec((None, d, r), lambda l, j: (l, 0, 0)),
            pl.BlockSpec((None, r, tn), lambda l, j: (l, 0, j)),
            pl.BlockSpec((None, 1, tn), lambda l, j: (l, 0, j)),
        ],
        out_specs=pl.BlockSpec((None, 8, tn), lambda l, j: (l, 0, j)),
        out_shape=jax.ShapeDtypeStruct((depth, 8, n), F32),
        scratch_shapes=[pltpu.VMEM((8, r), F32)],
        compiler_params=_params(("parallel", "arbitrary")),
        name="adaln_modulation",
    )(cv, ada_down, ada_up, ada_bias.reshape(depth, 1, n))


def _normmod_kernel(x_ref, g_ref, mod_ref, h_ref, *, seq, tm):
    x = x_ref[...]
    ctx = _is_ctx(pl.program_id(0) * tm, tm, seq)
    shift = jnp.where(ctx, mod_ref[1, 0:1, :], mod_ref[0, 0:1, :])
    scale = jnp.where(ctx, mod_ref[1, 1:2, :], mod_ref[0, 1:2, :])
    y = x * lax.rsqrt(jnp.mean(x * x, axis=-1, keepdims=True) + EPS)
    h_ref[...] = ((y * g_ref[...]) * (1.0 + scale) + shift).astype(h_ref.dtype)


def _normmod(xs, g, mod, seq):
    n, d = xs.shape
    tm = _row_tile(n, NORM_ROWS)
    return pl.pallas_call(
        functools.partial(_normmod_kernel, seq=seq, tm=tm),
        grid=(n // tm,),
        in_specs=[
            pl.BlockSpec((tm, d), lambda i: (i, 0)),
            pl.BlockSpec((1, d), lambda i: (0, 0)),
            pl.BlockSpec((2, 3, d), lambda i: (0, 0, 0)),
        ],
        out_specs=pl.BlockSpec((tm, d), lambda i: (i, 0)),
        out_shape=jax.ShapeDtypeStruct((n, d), BF16),
        compiler_params=_params(("parallel",)),
        name="norm_modulate",
    )(xs, g, mod)


def _ffn_up_head_kernel(h_ref, wa_ref, wb_ref, o_ref, wa_out, wb_out, *, tm):
    wa_out[...] = wa_ref[...].astype(BF16)
    wb_out[...] = wb_ref[...].astype(BF16)
    _ffn_up_tail_kernel(h_ref, wa_out, wb_out, None, o_ref, tm=tm)


def _ffn_up_tail_kernel(h_ref, wa_ref, wb_ref, _, o_ref, *, tm):
    def body(rows, _):
        h = h_ref[rows, :]
        a = _dot(h, wa_ref[...])
        b = _dot(h, wb_ref[...])
        o_ref[rows, :] = (a * _sigmoid(a) * b).astype(o_ref.dtype)

    _for_row_subtiles(tm, body)


def _ffn_up(h, w13, l, s):
    n, d = h.shape
    f = w13.shape[-1] // 2
    tm = _row_tile(n, DENSE_ROWS)
    th = min(f, HEAD_COLS)
    nh = f // th
    wcopy = jax.ShapeDtypeStruct((d, f), BF16)
    act, wa, wb = pl.pallas_call(
        functools.partial(_ffn_up_head_kernel, tm=tm),
        grid=(nh,),
        in_specs=[
            pl.BlockSpec((tm, d), lambda j: (0, 0), pipeline_mode=pl.Buffered(1)),
            pl.BlockSpec((None, None, d, th), lambda j: (l, s, 0, j)),
            pl.BlockSpec((None, None, d, th), lambda j: (l, s, 0, nh + j)),
        ],
        out_specs=[
            pl.BlockSpec((tm, th), lambda j: (0, j)),
            pl.BlockSpec((d, th), lambda j: (0, j)),
            pl.BlockSpec((d, th), lambda j: (0, j)),
        ],
        out_shape=[jax.ShapeDtypeStruct((n, f), BF16), wcopy, wcopy],
        compiler_params=_params(("arbitrary",)),
        name="ffn_up_head",
    )(h, w13, w13)
    if n == tm:
        return act
    tf = min(f, 512)
    return pl.pallas_call(
        functools.partial(_ffn_up_tail_kernel, tm=tm),
        grid=(n // tm - 1, f // tf),
        in_specs=[
            pl.BlockSpec((tm, d), lambda i, j: (i + 1, 0)),
            pl.BlockSpec((d, tf), lambda i, j: (0, j)),
            pl.BlockSpec((d, tf), lambda i, j: (0, j)),
            pl.BlockSpec(memory_space=pl.ANY),
        ],
        out_specs=pl.BlockSpec((tm, tf), lambda i, j: (i + 1, j)),
        out_shape=jax.ShapeDtypeStruct((n, f), BF16),
        input_output_aliases={3: 0},
        compiler_params=_params(("parallel", "arbitrary")),
        name="ffn_up_tail",
    )(h, wa, wb, act)


def _resid_kernel(a_ref, w_ref, x_ref, gate_ref, o_ref, *, tm, seq, coef):
    row_base = pl.program_id(0) * tm

    def body(rows, r0):
        sub = tm // ROW_SPLIT
        gate = jnp.where(_is_ctx(row_base + r0, sub, seq), gate_ref[1], gate_ref[0])
        o_ref[rows, :] = x_ref[rows, :] + (coef * gate) * _dot(a_ref[rows, :], w_ref[...])

    _for_row_subtiles(tm, body)


def _resid_proj(a, w, widx, xs, gate, seq, coef):
    n, d = xs.shape
    k = a.shape[1]
    tm = _row_tile(n, DENSE_ROWS)
    tn = min(d, 512)
    lead = (None,) * len(widx)
    return pl.pallas_call(
        functools.partial(_resid_kernel, tm=tm, seq=seq, coef=coef),
        grid=(n // tm, d // tn),
        in_specs=[
            pl.BlockSpec((tm, k), lambda i, j: (i, 0)),
            pl.BlockSpec(lead + (k, tn), lambda i, j: widx + (0, j)),
            pl.BlockSpec((tm, tn), lambda i, j: (i, j)),
            pl.BlockSpec((2, 1, tn), lambda i, j: (0, 0, j)),
        ],
        out_specs=pl.BlockSpec((tm, tn), lambda i, j: (i, j)),
        out_shape=jax.ShapeDtypeStruct((n, d), F32),
        compiler_params=_params(("parallel", "arbitrary")),
        name="gated_residual_proj",
    )(a, w, xs, gate)


def _proj_head_kernel(h_ref, wt_ref, wgt_ref, gb_ref, z_ref, zg_ref, wt_out, wgt_out, *, tm):
    wt_out[...] = wt_ref[0].astype(BF16)

    @pl.when(pl.program_id(0) == 0)
    def _():
        wgt_out[...] = wgt_ref[0].astype(BF16)

    _proj_rows(h_ref, wt_out, wgt_out, gb_ref, z_ref, zg_ref, tm, pl.program_id(0) == 0)


def _proj_tail_kernel(h_ref, wt_ref, wgt_ref, gb_ref, _z, _zg, z_ref, zg_ref, *, tm):
    _proj_rows(h_ref, wt_ref, wgt_ref, gb_ref, z_ref, zg_ref, tm, pl.program_id(1) == 0)


def _proj_rows(h_ref, wt_ref, wgt_ref, gb_ref, z_ref, zg_ref, tm, first_col_step):
    def body(rows, _):
        h = h_ref[rows, :]
        z_ref[rows, :] = _dot_nt(h, wt_ref[...]).astype(z_ref.dtype)

        @pl.when(first_col_step)
        def _():
            zg_ref[rows, :] = _dot_nt(h, wgt_ref[...]) + gb_ref[...]

    _for_row_subtiles(tm, body)


def _proj(h, w_t, gate_b, l):
    n, d = h.shape
    g0 = Z_MO + BRANCH_WIDTH
    nz = w_t.shape[1] - GATE_COLS
    tm = _row_tile(n, DENSE_ROWS)
    th = 2 * HEAD_COLS
    assert g0 % th == 0 and nz % th == 0
    outs = [jax.ShapeDtypeStruct((n, nz), BF16), jax.ShapeDtypeStruct((n, LANES), F32)]
    z, zg, wt, wgt = pl.pallas_call(
        functools.partial(_proj_head_kernel, tm=tm),
        grid=(nz // th,),
        in_specs=[
            pl.BlockSpec((tm, d), lambda j: (0, 0), pipeline_mode=pl.Buffered(1)),
            pl.BlockSpec((pl.Element(1), pl.Element(th), pl.Element(d)),
                         lambda j: (l, pl.multiple_of(
                             j * th + jnp.where(j * th >= g0, GATE_COLS, 0), GATE_COLS), 0)),
            pl.BlockSpec((pl.Element(1), pl.Element(LANES), pl.Element(d)), lambda j: (l, g0, 0)),
            pl.BlockSpec((1, LANES), lambda j: (0, 0)),
        ],
        out_specs=[
            pl.BlockSpec((tm, th), lambda j: (0, j)),
            pl.BlockSpec((tm, LANES), lambda j: (0, 0)),
            pl.BlockSpec((th, d), lambda j: (j, 0)),
            pl.BlockSpec((LANES, d), lambda j: (0, 0)),
        ],
        out_shape=outs + [jax.ShapeDtypeStruct((nz, d), BF16), jax.ShapeDtypeStruct((LANES, d), BF16)],
        compiler_params=_params(("arbitrary",)),
        name="mixer_in_proj_head",
    )(h, w_t, w_t, gate_b)
    if n == tm:
        return z, zg
    tn = 1024
    return pl.pallas_call(
        functools.partial(_proj_tail_kernel, tm=tm),
        grid=(n // tm - 1, nz // tn),
        in_specs=[
            pl.BlockSpec((tm, d), lambda i, j: (i + 1, 0)),
            pl.BlockSpec((tn, d), lambda i, j: (j, 0)),
            pl.BlockSpec((LANES, d), lambda i, j: (0, 0)),
            pl.BlockSpec((1, LANES), lambda i, j: (0, 0)),
            pl.BlockSpec(memory_space=pl.ANY),
            pl.BlockSpec(memory_space=pl.ANY),
        ],
        out_specs=[
            pl.BlockSpec((tm, tn), lambda i, j: (i + 1, j)),
            pl.BlockSpec((tm, LANES), lambda i, j: (i + 1, 0)),
        ],
        out_shape=outs,
        input_output_aliases={4: 0, 5: 1},
        compiler_params=_params(("parallel", "arbitrary")),
        name="mixer_in_proj_tail",
    )(h, wt, wgt, gate_b, z, zg)


def _gmlp_kernel(u_ref, v_ref, ng_ref, ws_ref, bst_ref, o_ref, *, tm):
    v = jax.nn.gelu(v_ref[...].astype(F32))
    vc = v - jnp.mean(v, axis=-1, keepdims=True)
    vn = vc * lax.rsqrt(jnp.mean(vc * vc, axis=-1, keepdims=True) + EPS) * ng_ref[...]
    vn = vn.astype(BF16)
    for c in range(tm // CHUNK):
        rows = slice(c * CHUNK, (c + 1) * CHUNK)
        for g in range(A_GROUPS):
            cols = slice(g * A_GDIM, (g + 1) * A_GDIM)
            mixed = _dot(ws_ref[g].astype(BF16), vn[rows, cols]) + bst_ref[:, g:g + 1]
            u = jax.nn.gelu(u_ref[rows, cols].astype(F32))
            o_ref[rows, cols] = (u * mixed).astype(o_ref.dtype)


def _gmlp(z, n, ng, ws, bst):
    tm = ELEM_ROWS
    bw = BRANCH_WIDTH
    return pl.pallas_call(
        functools.partial(_gmlp_kernel, tm=tm),
        grid=(n // tm,),
        in_specs=[
            pl.BlockSpec((tm, bw), lambda i: (i, Z_UV // bw)),
            pl.BlockSpec((tm, bw), lambda i: (i, Z_UV // bw + 1)),
            pl.BlockSpec((1, bw), lambda i: (0, 0)),
            pl.BlockSpec((A_GROUPS, CHUNK, CHUNK), lambda i: (0, 0, 0)),
            pl.BlockSpec((CHUNK, A_GROUPS), lambda i: (0, 0)),
        ],
        out_specs=pl.BlockSpec((tm, bw), lambda i: (i, 0)),
        out_shape=jax.ShapeDtypeStruct((n, bw), BF16),
        compiler_params=_params(("parallel",)),
        name="gmlp_branch",
    )(z, z, ng, ws, bst)


def _mlstm_head(rev, q, k, v, gi_col, gf_col, gi_row, gf_row, c_state, n_state, m_state):
    L = q.shape[0]
    r = lax.broadcasted_iota(jnp.int32, (L, L), 0)
    c = lax.broadcasted_iota(jnp.int32, (L, L), 1)
    seen = (c >= r) if rev else (c <= r)
    seen_t = (r >= c) if rev else (r <= c)
    lf_col = _log_sigmoid(gf_col)
    lf_row = _log_sigmoid(gf_row)
    b_col = jnp.sum(jnp.where(seen, lf_row, 0.0), axis=1, keepdims=True)
    b_row = jnp.sum(jnp.where(seen_t, lf_col, 0.0), axis=0, keepdims=True)
    b_last = jnp.sum(lf_row, axis=1, keepdims=True)
    qk = _dot_nt(q, k)
    yield

    dmat = jnp.where(seen, b_col - b_row + gi_row, -jnp.inf)
    a_col = b_col + m_state
    m_t = jnp.maximum(a_col, jnp.max(dmat, axis=1, keepdims=True))
    g_row = b_last - b_row + gi_row
    g_col = b_last - b_col + gi_col
    m_new = jnp.maximum(b_last + m_state, jnp.max(g_row, axis=1, keepdims=True))
    yield

    wa = jnp.exp(a_col - m_t)
    s = qk * jnp.exp(dmat - m_t)
    decay = jnp.exp(b_last + m_state - m_new)
    kw = k.astype(F32) * jnp.exp(g_col - m_new)
    yield

    sv = _dot(s.astype(BF16), v)
    qc = _dot(q, c_state.astype(BF16))
    kv = _dot(kw.T.astype(BF16), v)
    yield

    qn = jnp.sum(q.astype(F32) * n_state, axis=1, keepdims=True)
    den = wa * qn + jnp.sum(s, axis=1, keepdims=True)
    h = (wa * qc + sv) / jnp.maximum(jnp.abs(den), jnp.exp(-m_t))
    c_new = decay * c_state + kv
    n_new = decay * n_state + jnp.sum(kw, axis=0, keepdims=True)
    yield h, c_new, n_new, m_new


def _mlstm_kernel(qf_ref, kf_ref, vf_ref, gf_ref, qb_ref, kb_ref, vb_ref, gb_ref,
                  hf_ref, hb_ref, c_ref, n_ref, m_ref):
    @pl.when(pl.program_id(0) == 0)
    def _():
        c_ref[...] = jnp.zeros_like(c_ref)
        n_ref[...] = jnp.zeros_like(n_ref)
        m_ref[...] = jnp.zeros_like(m_ref)

    heads = []
    for d, (q_ref, k_ref, v_ref, g_ref, h_ref) in enumerate(
            ((qf_ref, kf_ref, vf_ref, gf_ref, hf_ref), (qb_ref, kb_ref, vb_ref, gb_ref, hb_ref))):
        gates = g_ref[...]
        gates_t = gates.T
        for hd in range(B_HEADS):
            idx = d * B_HEADS + hd
            ci = d * 2 * B_HEADS + hd
            cf = ci + B_HEADS
            qs = slice(hd * B_DQK, (hd + 1) * B_DQK)
            vs = slice(hd * B_DV, (hd + 1) * B_DV)
            q = (q_ref[:, qs].astype(F32) * (B_DQK ** -0.5)).astype(BF16)
            stages = _mlstm_head(
                d == 1, q, k_ref[:, qs], v_ref[:, vs],
                gates[:, ci:ci + 1], gates[:, cf:cf + 1],
                gates_t[ci:ci + 1, :], gates_t[cf:cf + 1, :],
                c_ref[idx], n_ref[idx], m_ref[idx][:, 0:1])
            heads.append((stages, idx, h_ref, vs))

    for _ in range(MLSTM_STAGES - 1):
        for stages, _, _, _ in heads:
            next(stages)
    for stages, idx, h_ref, vs in heads:
        h, c_new, n_new, m_new = next(stages)
        h_ref[:, vs] = h
        c_ref[idx] = c_new
        n_ref[idx] = n_new
        m_ref[idx] = jnp.broadcast_to(m_new, (1, LANES))


def _mlstm(z, zg, n_lat, n_ctx):
    n_steps = n_lat + n_ctx
    n = n_steps * CHUNK

    def fwd(t):
        return (t + n_lat) % n_steps

    def bwd(t):
        return n_steps - 1 - t

    def specs(order):
        return [
            pl.BlockSpec((CHUNK, B_HEADS * B_DQK), lambda t: (order(t), Z_MQ // (B_HEADS * B_DQK))),
            pl.BlockSpec((CHUNK, B_HEADS * B_DQK), lambda t: (order(t), Z_MK // (B_HEADS * B_DQK))),
            pl.BlockSpec((CHUNK, BRANCH_WIDTH), lambda t: (order(t), Z_MV // BRANCH_WIDTH)),
            pl.BlockSpec((CHUNK, LANES), lambda t: (order(t), 0)),
        ]

    return pl.pallas_call(
        _mlstm_kernel,
        grid=(n_steps,),
        in_specs=specs(fwd) + specs(bwd),
        out_specs=[
            pl.BlockSpec((CHUNK, BRANCH_WIDTH), lambda t: (fwd(t), 0)),
            pl.BlockSpec((CHUNK, BRANCH_WIDTH), lambda t: (bwd(t), 0)),
        ],
        out_shape=[jax.ShapeDtypeStruct((n, BRANCH_WIDTH), F32)] * 2,
        scratch_shapes=[
            pltpu.VMEM((2 * B_HEADS, B_DQK, B_DV), F32),
            pltpu.VMEM((2 * B_HEADS, 1, B_DQK), F32),
            pltpu.VMEM((2 * B_HEADS, 1, LANES), F32),
        ],
        compiler_params=_params(("arbitrary",)),
        name="mlstm_bidir",
    )(z, z, z, zg, z, z, z, zg)


def _mlstm_out_kernel(hf_ref, hb_ref, og_ref, g_ref, o_ref):
    for hd in range(B_HEADS):
        cols = slice(hd * B_DV, (hd + 1) * B_DV)
        h = hf_ref[:, cols] + hb_ref[:, cols]
        hn = h * lax.rsqrt(jnp.mean(h * h, axis=-1, keepdims=True) + EPS) * g_ref[:, cols]
        o_ref[:, cols] = (_sigmoid(og_ref[:, cols].astype(F32)) * hn).astype(o_ref.dtype)


def _mlstm_out(hf, hb, z, g):
    n = hf.shape[0]
    tm = ELEM_ROWS
    bw = BRANCH_WIDTH
    row = lambda i: (i, 0)
    return pl.pallas_call(
        _mlstm_out_kernel,
        grid=(n // tm,),
        in_specs=[
            pl.BlockSpec((tm, bw), row),
            pl.BlockSpec((tm, bw), row),
            pl.BlockSpec((tm, bw), lambda i: (i, Z_MO // bw)),
            pl.BlockSpec((1, bw), lambda i: (0, 0)),
        ],
        out_specs=pl.BlockSpec((tm, bw), row),
        out_shape=jax.ShapeDtypeStruct((n, bw), BF16),
        compiler_params=_params(("parallel",)),
        name="mlstm_out_gate",
    )(hf, hb, z, g)


def _rope_kernel(q_ref, k_ref, cos_ref, sin_ref, qo_ref, ko_ref):
    cos = cos_ref[...]
    sin = sin_ref[...]
    even = lax.broadcasted_iota(jnp.int32, cos.shape, 1) % 2 == 0
    for src, dst, mult in ((q_ref, qo_ref, C_DH ** -0.5 * LOG2E), (k_ref, ko_ref, None)):
        for g in range(2 * C_HEADS):
            cols = slice(g * C_DH, (g + 1) * C_DH)
            x = src[:, cols].astype(F32)
            partner = jnp.where(even, pltpu.roll(x, C_DH - 1, 1), pltpu.roll(x, 1, 1))
            y = x * cos + partner * sin
            if mult is not None:
                y = y * mult
            dst[:, cols] = y.astype(dst.dtype)


def _rope(z, cos_t, sin_t):
    n = cos_t.shape[0]
    tm = ELEM_ROWS
    w = 2 * C_HEADS * C_DH
    return pl.pallas_call(
        _rope_kernel,
        grid=(n // tm,),
        in_specs=[
            pl.BlockSpec((tm, w), lambda i: (i, Z_AQ // w)),
            pl.BlockSpec((tm, w), lambda i: (i, Z_AK // w)),
            pl.BlockSpec((tm, C_DH), lambda i: (i, 0)),
            pl.BlockSpec((tm, C_DH), lambda i: (i, 0)),
        ],
        out_specs=[pl.BlockSpec((tm, w), lambda i: (i, 0))] * 2,
        out_shape=[jax.ShapeDtypeStruct((n, w), BF16)] * 2,
        compiler_params=_params(("parallel",)),
        name="axial_rope",
    )(z, z, cos_t, sin_t)


def _attn_kernel(q_ref, k_ref, v_ref, lam_ref, ng_ref, o_ref,
                 m1_ref, l1_ref, a1_ref, m2_ref, l2_ref, a2_ref, *, nk, tk, lam_init):
    stats = ((m1_ref, l1_ref, a1_ref), (m2_ref, l2_ref, a2_ref))
    for m_ref, l_ref, a_ref in stats:
        m_ref[...] = jnp.full_like(m_ref, -jnp.inf)
        l_ref[...] = jnp.zeros_like(l_ref)
        a_ref[...] = jnp.zeros_like(a_ref)

    sub = q_ref.shape[0] // ATTN_ROW_SPLIT

    def block(keys, r, mp_i, m_ref, l_ref, a_ref):
        rows = slice(r * sub, (r + 1) * sub)
        cols = slice(mp_i * C_DH, (mp_i + 1) * C_DH)
        s = _dot_nt(q_ref[rows, cols], k_ref[keys, cols])
        yield
        m_prev = m_ref[rows, :]
        m_new = jnp.maximum(m_prev, jnp.max(s, axis=-1, keepdims=True))
        alpha = jnp.exp2(m_prev - m_new)
        m_ref[rows, :] = m_new
        yield
        p = jnp.exp2(s - m_new)
        l_ref[rows, :] = alpha * l_ref[rows, :] + jnp.sum(p, axis=-1, keepdims=True)
        yield
        pv = _dot(p.astype(BF16), v_ref[keys, :])
        yield
        a_ref[rows, :] = alpha * a_ref[rows, :] + pv
        yield

    def key_chunk(c, carry):
        keys = pl.ds(pl.multiple_of(c * tk, tk), tk)
        for r0 in range(0, ATTN_ROW_SPLIT, ATTN_GROUP):
            blocks = [block(keys, r, mp_i, *refs)
                      for r in range(r0, r0 + ATTN_GROUP) for mp_i, refs in enumerate(stats)]
            for _ in range(ATTN_STAGES):
                for b in blocks:
                    next(b)
        return carry

    lax.fori_loop(0, nk, key_chunk, 0, unroll=min(nk, ATTN_KEY_UNROLL))

    lp = lam_ref[...]
    lam = (jnp.exp(jnp.sum(lp[0:1] * lp[1:2], axis=-1, keepdims=True))
           - jnp.exp(jnp.sum(lp[2:3] * lp[3:4], axis=-1, keepdims=True)) + lam_init)
    o = a1_ref[...] / l1_ref[...] - lam * (a2_ref[...] / l2_ref[...])
    y = o * lax.rsqrt(jnp.mean(o * o, axis=-1, keepdims=True) + EPS) * ng_ref[...]
    o_ref[...] = (y * (1.0 - lam_init)).astype(o_ref.dtype)


def _attention(qr, kr, z, lam_p, ng, lam_init, q_row0, n_q_rows, kv_row0, n_kv_rows, tq, tk):
    nq = n_q_rows // tq
    nk = n_kv_rows // tk
    qb0 = q_row0 // tq
    kvb = kv_row0 // n_kv_rows
    assert kv_row0 % n_kv_rows == 0
    return pl.pallas_call(
        functools.partial(_attn_kernel, nk=nk, tk=tk, lam_init=lam_init),
        grid=(C_HEADS, nq),
        in_specs=[
            pl.BlockSpec((tq, C_DV), lambda h, i: (qb0 + i, h)),
            pl.BlockSpec((n_kv_rows, C_DV), lambda h, i: (kvb, h)),
            pl.BlockSpec((n_kv_rows, C_DV), lambda h, i: (kvb, Z_AV // C_DV + h)),
            pl.BlockSpec((4, C_DH), lambda h, i: (0, 0)),
            pl.BlockSpec((1, C_DV), lambda h, i: (0, 0)),
        ],
        out_specs=pl.BlockSpec((tq, C_DV), lambda h, i: (i, h)),
        out_shape=jax.ShapeDtypeStruct((n_q_rows, C_HEADS * C_DV), BF16),
        scratch_shapes=[
            pltpu.VMEM((tq, 1), F32), pltpu.VMEM((tq, 1), F32), pltpu.VMEM((tq, C_DV), F32),
            pltpu.VMEM((tq, 1), F32), pltpu.VMEM((tq, 1), F32), pltpu.VMEM((tq, C_DV), F32),
        ],
        compiler_params=_params(("parallel", "arbitrary")),
        name="diff_attention",
    )(qr, kr, z, lam_p, ng)


def _merge_kernel(ya_ref, yb_ref, yc_ref, g0_ref, g1_ref, g2_ref, wb_ref, y_ref, *, tm):
    def body(rows, _):
        y = _sigmoid(g0_ref[rows, :].astype(F32)) * _dot(ya_ref[rows, :], wb_ref[0])
        y += _sigmoid(g1_ref[rows, :].astype(F32)) * _dot(yb_ref[rows, :], wb_ref[1])
        y += _sigmoid(g2_ref[rows, :].astype(F32)) * _dot(yc_ref[rows, :], wb_ref[2])
        y_ref[rows, :] = y.astype(y_ref.dtype)

    _for_row_subtiles(tm, body)


def _merge(ya, yb, yc, z, w_branch, l):
    n = ya.shape[0]
    d = w_branch.shape[-1]
    tm = _row_tile(n, DENSE_ROWS)
    tn = min(d, 512)
    bw = BRANCH_WIDTH
    gate_blk = Z_GATE // tn
    row = lambda i, j: (i, 0)
    return pl.pallas_call(
        functools.partial(_merge_kernel, tm=tm),
        grid=(n // tm, d // tn),
        in_specs=[
            pl.BlockSpec((tm, bw), row),
            pl.BlockSpec((tm, bw), row),
            pl.BlockSpec((tm, bw), row),
            pl.BlockSpec((tm, tn), lambda i, j: (i, gate_blk + j)),
            pl.BlockSpec((tm, tn), lambda i, j: (i, gate_blk + d // tn + j)),
            pl.BlockSpec((tm, tn), lambda i, j: (i, gate_blk + 2 * (d // tn) + j)),
            pl.BlockSpec((None, N_BRANCH, bw, tn), lambda i, j: (l, 0, 0, j)),
        ],
        out_specs=pl.BlockSpec((tm, tn), lambda i, j: (i, j)),
        out_shape=jax.ShapeDtypeStruct((n, d), BF16),
        compiler_params=_params(("parallel", "arbitrary")),
        name="branch_merge",
    )(ya, yb, yc, z, z, z, w_branch)


def _final_kernel(x_ref, g_ref, o_ref):
    x = x_ref[...]
    o_ref[...] = x * lax.rsqrt(jnp.mean(x * x, axis=-1, keepdims=True) + EPS) * g_ref[...]


def _final_norm(xs, g, n_rows):
    d = xs.shape[1]
    tm = ELEM_ROWS
    return pl.pallas_call(
        _final_kernel,
        grid=(n_rows // tm,),
        in_specs=[pl.BlockSpec((tm, d), lambda i: (i, 0)), pl.BlockSpec((1, d), lambda i: (0, 0))],
        out_specs=pl.BlockSpec((tm, d), lambda i: (i, 0)),
        out_shape=jax.ShapeDtypeStruct((n_rows, d), F32),
        compiler_params=_params(("parallel",)),
        name="final_norm",
    )(xs, g)


def _rope_tables(n_lat, n_rows):
    t = jnp.arange(n_lat)
    r = (t // GRID_W).astype(F32)
    col = (t % GRID_W).astype(F32)
    n_freq = C_DH // 4
    inv = ROPE_THETA ** (-jnp.arange(n_freq, dtype=F32) / n_freq)
    ang = jnp.concatenate([r[:, None] * inv, col[:, None] * inv], axis=-1)
    cos = jnp.repeat(jnp.cos(ang), 2, axis=-1)
    sin = jnp.stack([-jnp.sin(ang), jnp.sin(ang)], axis=-1).reshape(n_lat, C_DH)
    pad = n_rows - n_lat
    cos = jnp.concatenate([cos, jnp.ones((pad, C_DH), F32)], axis=0)
    sin = jnp.concatenate([sin, jnp.zeros((pad, C_DH), F32)], axis=0)
    return cos, sin


def kernel(x, c, ctx, c_ctx, ada_down, ada_up, ada_bias, norm_g, ffn_w13, ffn_w2, w_in, gmlp_norm_g,
           gmlp_ws, gmlp_bs, mlstm_gate_b, mlstm_norm_g, diff_lambda, diff_norm_g, w_branch, w_out,
           final_g):
    bsz, seq, d = x.shape
    n_ctx = ctx.shape[1]
    depth = ada_down.shape[0]
    n_tok = seq + n_ctx
    assert bsz == 1 and seq % ELEM_ROWS == 0 and n_ctx % ELEM_ROWS == 0
    assert w_in.shape[-1] == Z_GATE + GATE_COLS + N_BRANCH * d

    xs = jnp.concatenate([x[0], ctx[0]], axis=0)
    w2 = ffn_w2.astype(BF16)
    w_in_t = jnp.swapaxes(w_in, 1, 2)
    wbr = w_branch.astype(BF16)
    wo = w_out.astype(BF16)
    cos_t, sin_t = _rope_tables(seq, n_tok)
    gate_b = jnp.pad(mlstm_gate_b.reshape(depth, 1, GATE_COLS), ((0, 0), (0, 0), (0, LANES - GATE_COLS)))
    cv = jnp.concatenate([c, c_ctx[None], jnp.zeros((6, d), F32)], axis=0)

    mod = _modulation(cv, ada_down, ada_up, ada_bias)[:, :2].reshape(depth, 2, N_SUB, 3, d)

    tq = 1024 if seq % 1024 == 0 else ELEM_ROWS
    tk = 1408 if n_tok % 1408 == 0 else ELEM_ROWS
    for l in range(depth):
        lam_init = 0.8 - 0.6 * math.exp(-0.3 * l)

        def ffn(xs, s, sub):
            h = _normmod(xs, norm_g[l, sub:sub + 1], mod[l, :, sub], seq)
            act = _ffn_up(h, ffn_w13, l, s)
            return _resid_proj(act, w2, (l, s), xs, mod[l, :, sub, 2:3], seq, 0.5)

        xs = ffn(xs, 0, 0)

        h = _normmod(xs, norm_g[l, 1:2], mod[l, :, 1], seq)
        z, zg = _proj(h, w_in_t, gate_b[l], l)
        ya = _gmlp(z, n_tok, gmlp_norm_g[l][None], gmlp_ws[l], gmlp_bs[l].T)
        hf, hb = _mlstm(z, zg, seq // CHUNK, n_ctx // CHUNK)
        yb = _mlstm_out(hf, hb, z, mlstm_norm_g[l][None])
        qr, kr = _rope(z, cos_t, sin_t)
        ng = diff_norm_g[l][None]
        yc_lat = _attention(qr, kr, z, diff_lambda[l], ng, lam_init, 0, seq, 0, n_tok, tq, tk)
        yc_ctx = _attention(qr, kr, z, diff_lambda[l], ng, lam_init, seq, n_ctx, seq, n_ctx, n_ctx, n_ctx)
        yc = jnp.concatenate([yc_lat, yc_ctx], axis=0)
        y = _merge(ya, yb, yc, z, wbr, l)
        xs = _resid_proj(y, wo, (l,), xs, mod[l, :, 1, 2:3], seq, 1.0)

        xs = ffn(xs, 1, 2)
    return _final_norm(xs, final_g[None], seq)[None]
```

```python
import functools
import math

import jax
import jax.numpy as jnp
from jax import lax
from jax.experimental import pallas as pl
from jax.experimental.pallas import tpu as pltpu

F32 = jnp.float32
BF16 = jnp.bfloat16

GRID_W = 64
CHUNK = 128
BRANCH_WIDTH = 1024
A_GROUPS = 8
A_GDIM = BRANCH_WIDTH // A_GROUPS
B_HEADS = 4
B_DQK = 128
B_DV = BRANCH_WIDTH // B_HEADS
C_HEADS = 4
C_DH = 128
C_DV = 2 * C_DH
N_BRANCH = 3
N_SUB = 3
ROPE_THETA = 10000.0
EPS = 1e-6
GATE_COLS = 4 * B_HEADS
LANES = 128
Z_UV, Z_MQ, Z_MK, Z_MV, Z_MO, Z_AQ, Z_AK, Z_AV, Z_GATE = (
    0, 2048, 2560, 3072, 4096, 5120, 6144, 7168, 8192)
DENSE_ROWS = 1408
ROW_SPLIT = 2
HEAD_COLS = 256
ELEM_ROWS = 256
MLSTM_STAGES = 5
ATTN_STAGES = 5
ATTN_KEY_UNROLL = 3
ATTN_GROUP = 2
ATTN_ROW_SPLIT = 4
VMEM_LIMIT = 56 * 1024 * 1024
LOG2E = math.log2(math.e)


def _params(sem):
    return pltpu.CompilerParams(dimension_semantics=sem, vmem_limit_bytes=VMEM_LIMIT)


def _dot(a, b):
    return jnp.dot(a, b, preferred_element_type=F32)


def _dot_nt(a, b):
    return lax.dot_general(a, b, (((1,), (1,)), ((), ())), preferred_element_type=F32)


def _sigmoid(x):
    return 0.5 * jnp.tanh(0.5 * x) + 0.5


def _log_sigmoid(x):
    return jnp.minimum(x, 0.0) - jnp.log(1.0 + jnp.exp(-jnp.abs(x)))


def _row_tile(n, cap):
    unit = 16 * ROW_SPLIT
    return max(t for t in range(unit, min(n, cap) + 1, unit) if n % t == 0)


def _for_row_subtiles(tm, body):
    sub = tm // ROW_SPLIT

    def step(r, carry):
        body(pl.ds(pl.multiple_of(r * sub, sub), sub), r * sub)
        return carry

    lax.fori_loop(0, ROW_SPLIT, step, 0)


def _is_ctx(row0, n, seq):
    return row0 + lax.broadcasted_iota(jnp.int32, (n, 1), 0) >= seq


def _mod_kernel(cv_ref, down_ref, up_ref, b_ref, o_ref, t_ref):
    @pl.when(pl.program_id(1) == 0)
    def _():
        s = cv_ref[...]
        s = s * _sigmoid(s)
        t_ref[...] = _dot(s.astype(BF16), down_ref[...].astype(BF16))

    o_ref[...] = _dot(t_ref[...].astype(BF16), up_ref[...].astype(BF16)) + b_ref[...]


def _modulation(cv, ada_down, ada_up, ada_bias):
    depth, d, r = ada_down.shape
    n = ada_up.shape[-1]
    tn = 3 * d
    return pl.pallas_call(
        _mod_kernel,
        grid=(depth, n // tn),
        in_specs=[
            pl.BlockSpec((8, d), lambda l, j: (0, 0)),
            pl.BlockSpec((None, d, r), lambda l, j: (l, 0, 0)),
            pl.BlockSpec((None, r, tn), lambda l, j: (l, 0, j)),
            pl.BlockSpec((None, 1, tn), lambda l, j: (l, 0, j)),
        ],
        out_specs=pl.BlockSpec((None, 8, tn), lambda l, j: (l, 0, j)),
        out_shape=jax.ShapeDtypeStruct((depth, 8, n), F32),
        scratch_shapes=[pltpu.VMEM((8, r), F32)],
        compiler_params=_params(("parallel", "arbitrary")),
        name="adaln_modulation",
    )(cv, ada_down, ada_up, ada_bias.reshape(depth, 1, n))


def _normmod_kernel(x_ref, g_ref, mod_ref, h_ref, *, seq, tm):
    stream = (pl.program_id(0) * tm >= seq).astype(jnp.int32)
    m = mod_ref[stream]
    gain = g_ref[...] * (1.0 + m[1:2, :])
    x = x_ref[...]
    y = x * lax.rsqrt(jnp.mean(x * x, axis=-1, keepdims=True) + EPS)
    h_ref[...] = (y * gain + m[0:1, :]).astype(h_ref.dtype)


def _normmod(xs, g, mod, seq):
    n, d = xs.shape
    tm = ELEM_ROWS
    assert seq % tm == 0 and n % tm == 0
    return pl.pallas_call(
        functools.partial(_normmod_kernel, seq=seq, tm=tm),
        grid=(n // tm,),
        in_specs=[
            pl.BlockSpec((tm, d), lambda i: (i, 0)),
            pl.BlockSpec((1, d), lambda i: (0, 0)),
            pl.BlockSpec((2, 3, d), lambda i: (0, 0, 0)),
        ],
        out_specs=pl.BlockSpec((tm, d), lambda i: (i, 0)),
        out_shape=jax.ShapeDtypeStruct((n, d), BF16),
        compiler_params=_params(("parallel",)),
        name="norm_modulate",
    )(xs, g, mod)


def _ffn_up_head_kernel(h_ref, wa_ref, wb_ref, o_ref, wa_out, wb_out, *, tm):
    wa_out[...] = wa_ref[...].astype(BF16)
    wb_out[...] = wb_ref[...].astype(BF16)
    _ffn_up_tail_kernel(h_ref, wa_out, wb_out, None, o_ref, tm=tm)


def _ffn_up_tail_kernel(h_ref, wa_ref, wb_ref, _, o_ref, *, tm):
    def body(rows, _):
        h = h_ref[rows, :]
        a = _dot(h, wa_ref[...])
        b = _dot(h, wb_ref[...])
        o_ref[rows, :] = (a * _sigmoid(a) * b).astype(o_ref.dtype)

    _for_row_subtiles(tm, body)


def _ffn_up(h, w13, l, s):
    n, d = h.shape
    f = w13.shape[-1] // 2
    tm = _row_tile(n, DENSE_ROWS)
    th = min(f, HEAD_COLS)
    nh = f // th
    wcopy = jax.ShapeDtypeStruct((d, f), BF16)
    act, wa, wb = pl.pallas_call(
        functools.partial(_ffn_up_head_kernel, tm=tm),
        grid=(nh,),
        in_specs=[
            pl.BlockSpec((tm, d), lambda j: (0, 0), pipeline_mode=pl.Buffered(1)),
            pl.BlockSpec((None, None, d, th), lambda j: (l, s, 0, j)),
            pl.BlockSpec((None, None, d, th), lambda j: (l, s, 0, nh + j)),
        ],
        out_specs=[
            pl.BlockSpec((tm, th), lambda j: (0, j)),
            pl.BlockSpec((d, th), lambda j: (0, j)),
            pl.BlockSpec((d, th), lambda j: (0, j)),
        ],
        out_shape=[jax.ShapeDtypeStruct((n, f), BF16), wcopy, wcopy],
        compiler_params=_params(("arbitrary",)),
        name="ffn_up_head",
    )(h, w13, w13)
    if n == tm:
        return act
    tf = min(f, 512)
    return pl.pallas_call(
        functools.partial(_ffn_up_tail_kernel, tm=tm),
        grid=(n // tm - 1, f // tf),
        in_specs=[
            pl.BlockSpec((tm, d), lambda i, j: (i + 1, 0)),
            pl.BlockSpec((d, tf), lambda i, j: (0, j)),
            pl.BlockSpec((d, tf), lambda i, j: (0, j)),
            pl.BlockSpec(memory_space=pl.ANY),
        ],
        out_specs=pl.BlockSpec((tm, tf), lambda i, j: (i + 1, j)),
        out_shape=jax.ShapeDtypeStruct((n, f), BF16),
        input_output_aliases={3: 0},
        compiler_params=_params(("parallel", "arbitrary")),
        name="ffn_up_tail",
    )(h, wa, wb, act)


def _resid_kernel(a_ref, w_ref, x_ref, gate_ref, o_ref, *, tm, seq, coef):
    row_base = pl.program_id(0) * tm

    def body(rows, r0):
        sub = tm // ROW_SPLIT
        gate = jnp.where(_is_ctx(row_base + r0, sub, seq), gate_ref[1], gate_ref[0])
        o_ref[rows, :] = x_ref[rows, :] + (coef * gate) * _dot(a_ref[rows, :], w_ref[...])

    _for_row_subtiles(tm, body)


def _resid_proj(a, w, widx, xs, gate, seq, coef):
    n, d = xs.shape
    k = a.shape[1]
    tm = _row_tile(n, DENSE_ROWS)
    tn = min(d, 512)
    lead = (None,) * len(widx)
    return pl.pallas_call(
        functools.partial(_resid_kernel, tm=tm, seq=seq, coef=coef),
        grid=(n // tm, d // tn),
        in_specs=[
            pl.BlockSpec((tm, k), lambda i, j: (i, 0)),
            pl.BlockSpec(lead + (k, tn), lambda i, j: widx + (0, j)),
            pl.BlockSpec((tm, tn), lambda i, j: (i, j)),
            pl.BlockSpec((2, 1, tn), lambda i, j: (0, 0, j)),
        ],
        out_specs=pl.BlockSpec((tm, tn), lambda i, j: (i, j)),
        out_shape=jax.ShapeDtypeStruct((n, d), F32),
        compiler_params=_params(("parallel", "arbitrary")),
        name="gated_residual_proj",
    )(a, w, xs, gate)


def _proj_head_kernel(h_ref, wt_ref, wgt_ref, gb_ref, z_ref, zg_ref, wt_out, wgt_out, *, tm):
    wt_out[...] = wt_ref[0].astype(BF16)

    @pl.when(pl.program_id(0) == 0)
    def _():
        wgt_out[...] = wgt_ref[0].astype(BF16)

    _proj_rows(h_ref, wt_out, wgt_out, gb_ref, z_ref, zg_ref, tm, pl.program_id(0) == 0)


def _proj_tail_kernel(h_ref, wt_ref, wgt_ref, gb_ref, _z, _zg, z_ref, zg_ref, *, tm):
    _proj_rows(h_ref, wt_ref, wgt_ref, gb_ref, z_ref, zg_ref, tm, pl.program_id(1) == 0)


def _proj_rows(h_ref, wt_ref, wgt_ref, gb_ref, z_ref, zg_ref, tm, first_col_step):
    def body(rows, _):
        h = h_ref[rows, :]
        z_ref[rows, :] = _dot_nt(h, wt_ref[...]).astype(z_ref.dtype)

        @pl.when(first_col_step)
        def _():
            zg_ref[rows, :] = _dot_nt(h, wgt_ref[...]) + gb_ref[...]

    _for_row_subtiles(tm, body)


def _proj(h, w_t, gate_b, l):
    n, d = h.shape
    g0 = Z_MO + BRANCH_WIDTH
    nz = w_t.shape[1] - GATE_COLS
    tm = _row_tile(n, DENSE_ROWS)
    th = 2 * HEAD_COLS
    assert g0 % th == 0 and nz % th == 0
    outs = [jax.ShapeDtypeStruct((n, nz), BF16), jax.ShapeDtypeStruct((n, LANES), F32)]
    z, zg, wt, wgt = pl.pallas_call(
        functools.partial(_proj_head_kernel, tm=tm),
        grid=(nz // th,),
        in_specs=[
            pl.BlockSpec((tm, d), lambda j: (0, 0), pipeline_mode=pl.Buffered(1)),
            pl.BlockSpec((pl.Element(1), pl.Element(th), pl.Element(d)),
                         lambda j: (l, pl.multiple_of(
                             j * th + jnp.where(j * th >= g0, GATE_COLS, 0), GATE_COLS), 0)),
            pl.BlockSpec((pl.Element(1), pl.Element(LANES), pl.Element(d)), lambda j: (l, g0, 0)),
            pl.BlockSpec((1, LANES), lambda j: (0, 0)),
        ],
        out_specs=[
            pl.BlockSpec((tm, th), lambda j: (0, j)),
            pl.BlockSpec((tm, LANES), lambda j: (0, 0)),
            pl.BlockSpec((th, d), lambda j: (j, 0)),
            pl.BlockSpec((LANES, d), lambda j: (0, 0)),
        ],
        out_shape=outs + [jax.ShapeDtypeStruct((nz, d), BF16), jax.ShapeDtypeStruct((LANES, d), BF16)],
        compiler_params=_params(("arbitrary",)),
        name="mixer_in_proj_head",
    )(h, w_t, w_t, gate_b)
    if n == tm:
        return z, zg
    tn = 1024
    return pl.pallas_call(
        functools.partial(_proj_tail_kernel, tm=tm),
        grid=(n // tm - 1, nz // tn),
        in_specs=[
            pl.BlockSpec((tm, d), lambda i, j: (i + 1, 0)),
            pl.BlockSpec((tn, d), lambda i, j: (j, 0)),
            pl.BlockSpec((LANES, d), lambda i, j: (0, 0)),
            pl.BlockSpec((1, LANES), lambda i, j: (0, 0)),
            pl.BlockSpec(memory_space=pl.ANY),
            pl.BlockSpec(memory_space=pl.ANY),
        ],
        out_specs=[
            pl.BlockSpec((tm, tn), lambda i, j: (i + 1, j)),
            pl.BlockSpec((tm, LANES), lambda i, j: (i + 1, 0)),
        ],
        out_shape=outs,
        input_output_aliases={4: 0, 5: 1},
        compiler_params=_params(("parallel", "arbitrary")),
        name="mixer_in_proj_tail",
    )(h, wt, wgt, gate_b, z, zg)


def _gmlp_kernel(u_ref, v_ref, ng_ref, ws_ref, bst_ref, o_ref, *, tm):
    v = jax.nn.gelu(v_ref[...].astype(F32))
    vc = v - jnp.mean(v, axis=-1, keepdims=True)
    vn = vc * lax.rsqrt(jnp.mean(vc * vc, axis=-1, keepdims=True) + EPS) * ng_ref[...]
    vn = vn.astype(BF16)
    for c in range(tm // CHUNK):
        rows = slice(c * CHUNK, (c + 1) * CHUNK)
        for g in range(A_GROUPS):
            cols = slice(g * A_GDIM, (g + 1) * A_GDIM)
            mixed = _dot(ws_ref[g].astype(BF16), vn[rows, cols]) + bst_ref[:, g:g + 1]
            u = jax.nn.gelu(u_ref[rows, cols].astype(F32))
            o_ref[rows, cols] = (u * mixed).astype(o_ref.dtype)


def _gmlp(z, n, ng, ws, bst):
    tm = ELEM_ROWS
    bw = BRANCH_WIDTH
    return pl.pallas_call(
        functools.partial(_gmlp_kernel, tm=tm),
        grid=(n // tm,),
        in_specs=[
            pl.BlockSpec((tm, bw), lambda i: (i, Z_UV // bw)),
            pl.BlockSpec((tm, bw), lambda i: (i, Z_UV // bw + 1)),
            pl.BlockSpec((1, bw), lambda i: (0, 0)),
            pl.BlockSpec((A_GROUPS, CHUNK, CHUNK), lambda i: (0, 0, 0)),
            pl.BlockSpec((CHUNK, A_GROUPS), lambda i: (0, 0)),
        ],
        out_specs=pl.BlockSpec((tm, bw), lambda i: (i, 0)),
        out_shape=jax.ShapeDtypeStruct((n, bw), BF16),
        compiler_params=_params(("parallel",)),
        name="gmlp_branch",
    )(z, z, ng, ws, bst)


def _mlstm_head(rev, q, k, v, gi_col, gf_col, gi_row, gf_row, c_state, n_state, m_state):
    L = q.shape[0]
    r = lax.broadcasted_iota(jnp.int32, (L, L), 0)
    c = lax.broadcasted_iota(jnp.int32, (L, L), 1)
    seen = (c >= r) if rev else (c <= r)
    seen_t = (r >= c) if rev else (r <= c)
    lf_col = _log_sigmoid(gf_col)
    lf_row = _log_sigmoid(gf_row)
    b_col = jnp.sum(jnp.where(seen, lf_row, 0.0), axis=1, keepdims=True)
    b_row = jnp.sum(jnp.where(seen_t, lf_col, 0.0), axis=0, keepdims=True)
    b_last = jnp.sum(lf_row, axis=1, keepdims=True)
    qk = _dot_nt(q, k)
    yield

    dmat = jnp.where(seen, b_col - b_row + gi_row, -jnp.inf)
    a_col = b_col + m_state
    m_t = jnp.maximum(a_col, jnp.max(dmat, axis=1, keepdims=True))
    g_row = b_last - b_row + gi_row
    g_col = b_last - b_col + gi_col
    m_new = jnp.maximum(b_last + m_state, jnp.max(g_row, axis=1, keepdims=True))
    yield

    wa = jnp.exp(a_col - m_t)
    s = qk * jnp.exp(dmat - m_t)
    decay = jnp.exp(b_last + m_state - m_new)
    kw = k.astype(F32) * jnp.exp(g_col - m_new)
    yield

    sv = _dot(s.astype(BF16), v)
    qc = _dot(q, c_state.astype(BF16))
    kv = lax.dot_general(kw.astype(BF16), v, (((0,), (0,)), ((), ())), preferred_element_type=F32)
    yield

    qn = jnp.sum(q.astype(F32) * n_state, axis=1, keepdims=True)
    den = wa * qn + jnp.sum(s, axis=1, keepdims=True)
    h = (wa * qc + sv) / jnp.maximum(jnp.abs(den), jnp.exp(-m_t))
    c_new = decay * c_state + kv
    n_new = decay * n_state + jnp.sum(kw, axis=0, keepdims=True)
    yield h, c_new, n_new, m_new


def _mlstm_kernel(qf_ref, kf_ref, vf_ref, gf_ref, qb_ref, kb_ref, vb_ref, gb_ref,
                  hf_ref, hb_ref, c_ref, n_ref, m_ref):
    @pl.when(pl.program_id(0) == 0)
    def _():
        c_ref[...] = jnp.zeros_like(c_ref)
        n_ref[...] = jnp.zeros_like(n_ref)
        m_ref[...] = jnp.zeros_like(m_ref)

    heads = []
    for d, (q_ref, k_ref, v_ref, g_ref, h_ref) in enumerate(
            ((qf_ref, kf_ref, vf_ref, gf_ref, hf_ref), (qb_ref, kb_ref, vb_ref, gb_ref, hb_ref))):
        gates = g_ref[...]
        gates_t = gates.T
        for hd in range(B_HEADS):
            idx = d * B_HEADS + hd
            ci = d * 2 * B_HEADS + hd
            cf = ci + B_HEADS
            qs = slice(hd * B_DQK, (hd + 1) * B_DQK)
            vs = slice(hd * B_DV, (hd + 1) * B_DV)
            q = (q_ref[:, qs].astype(F32) * (B_DQK ** -0.5)).astype(BF16)
            stages = _mlstm_head(
                d == 1, q, k_ref[:, qs], v_ref[:, vs],
                gates[:, ci:ci + 1], gates[:, cf:cf + 1],
                gates_t[ci:ci + 1, :], gates_t[cf:cf + 1, :],
                c_ref[idx], n_ref[idx], m_ref[idx][:, 0:1])
            heads.append((stages, idx, h_ref, vs))

    for _ in range(MLSTM_STAGES - 1):
        for stages, _, _, _ in heads:
            next(stages)
    for stages, idx, h_ref, vs in heads:
        h, c_new, n_new, m_new = next(stages)
        h_ref[:, vs] = h
        c_ref[idx] = c_new
        n_ref[idx] = n_new
        m_ref[idx] = jnp.broadcast_to(m_new, (1, LANES))


def _mlstm(z, zg, n_lat, n_ctx):
    n_steps = n_lat + n_ctx
    n = n_steps * CHUNK

    def fwd(t):
        return (t + n_lat) % n_steps

    def bwd(t):
        return n_steps - 1 - t

    def specs(order):
        return [
            pl.BlockSpec((CHUNK, B_HEADS * B_DQK), lambda t: (order(t), Z_MQ // (B_HEADS * B_DQK))),
            pl.BlockSpec((CHUNK, B_HEADS * B_DQK), lambda t: (order(t), Z_MK // (B_HEADS * B_DQK))),
            pl.BlockSpec((CHUNK, BRANCH_WIDTH), lambda t: (order(t), Z_MV // BRANCH_WIDTH)),
            pl.BlockSpec((CHUNK, LANES), lambda t: (order(t), 0)),
        ]

    return pl.pallas_call(
        _mlstm_kernel,
        grid=(n_steps,),
        in_specs=specs(fwd) + specs(bwd),
        out_specs=[
            pl.BlockSpec((CHUNK, BRANCH_WIDTH), lambda t: (fwd(t), 0)),
            pl.BlockSpec((CHUNK, BRANCH_WIDTH), lambda t: (bwd(t), 0)),
        ],
        out_shape=[jax.ShapeDtypeStruct((n, BRANCH_WIDTH), F32)] * 2,
        scratch_shapes=[
            pltpu.VMEM((2 * B_HEADS, B_DQK, B_DV), F32),
            pltpu.VMEM((2 * B_HEADS, 1, B_DQK), F32),
            pltpu.VMEM((2 * B_HEADS, 1, LANES), F32),
        ],
        compiler_params=_params(("arbitrary",)),
        name="mlstm_bidir",
    )(z, z, z, zg, z, z, z, zg)


def _mlstm_out_kernel(hf_ref, hb_ref, og_ref, g_ref, o_ref):
    for hd in range(B_HEADS):
        cols = slice(hd * B_DV, (hd + 1) * B_DV)
        h = hf_ref[:, cols] + hb_ref[:, cols]
        hn = h * lax.rsqrt(jnp.mean(h * h, axis=-1, keepdims=True) + EPS) * g_ref[:, cols]
        o_ref[:, cols] = (_sigmoid(og_ref[:, cols].astype(F32)) * hn).astype(o_ref.dtype)


def _mlstm_out(hf, hb, z, g):
    n = hf.shape[0]
    tm = ELEM_ROWS
    bw = BRANCH_WIDTH
    row = lambda i: (i, 0)
    return pl.pallas_call(
        _mlstm_out_kernel,
        grid=(n // tm,),
        in_specs=[
            pl.BlockSpec((tm, bw), row),
            pl.BlockSpec((tm, bw), row),
            pl.BlockSpec((tm, bw), lambda i: (i, Z_MO // bw)),
            pl.BlockSpec((1, bw), lambda i: (0, 0)),
        ],
        out_specs=pl.BlockSpec((tm, bw), row),
        out_shape=jax.ShapeDtypeStruct((n, bw), BF16),
        compiler_params=_params(("parallel",)),
        name="mlstm_out_gate",
    )(hf, hb, z, g)


def _rope_kernel(q_ref, k_ref, cos_ref, sin_ref, qo_ref, ko_ref):
    cos = cos_ref[...]
    sin = sin_ref[...]
    even = lax.broadcasted_iota(jnp.int32, cos.shape, 1) % 2 == 0
    for src, dst, mult in ((q_ref, qo_ref, C_DH ** -0.5 * LOG2E), (k_ref, ko_ref, None)):
        for g in range(2 * C_HEADS):
            cols = slice(g * C_DH, (g + 1) * C_DH)
            x = src[:, cols].astype(F32)
            partner = jnp.where(even, pltpu.roll(x, C_DH - 1, 1), pltpu.roll(x, 1, 1))
            y = x * cos + partner * sin
            if mult is not None:
                y = y * mult
            dst[:, cols] = y.astype(dst.dtype)


def _rope(z, cos_t, sin_t):
    n = cos_t.shape[0]
    tm = ELEM_ROWS
    w = 2 * C_HEADS * C_DH
    return pl.pallas_call(
        _rope_kernel,
        grid=(n // tm,),
        in_specs=[
            pl.BlockSpec((tm, w), lambda i: (i, Z_AQ // w)),
            pl.BlockSpec((tm, w), lambda i: (i, Z_AK // w)),
            pl.BlockSpec((tm, C_DH), lambda i: (i, 0)),
            pl.BlockSpec((tm, C_DH), lambda i: (i, 0)),
        ],
        out_specs=[pl.BlockSpec((tm, w), lambda i: (i, 0))] * 2,
        out_shape=[jax.ShapeDtypeStruct((n, w), BF16)] * 2,
        compiler_params=_params(("parallel",)),
        name="axial_rope",
    )(z, z, cos_t, sin_t)


def _attn_kernel(q_ref, k_ref, v_ref, lam_ref, ng_ref, o_ref,
                 m1_ref, l1_ref, a1_ref, m2_ref, l2_ref, a2_ref, *, nk, tk, lam_init):
    stats = ((m1_ref, l1_ref, a1_ref), (m2_ref, l2_ref, a2_ref))
    for m_ref, l_ref, a_ref in stats:
        m_ref[...] = jnp.full_like(m_ref, -jnp.inf)
        l_ref[...] = jnp.zeros_like(l_ref)
        a_ref[...] = jnp.zeros_like(a_ref)

    sub = q_ref.shape[0] // ATTN_ROW_SPLIT

    def block(keys, r, mp_i, m_ref, l_ref, a_ref):
        rows = slice(r * sub, (r + 1) * sub)
        cols = slice(mp_i * C_DH, (mp_i + 1) * C_DH)
        s = _dot_nt(q_ref[rows, cols], k_ref[keys, cols])
        yield
        m_prev = m_ref[rows, :]
        m_new = jnp.maximum(m_prev, jnp.max(s, axis=-1, keepdims=True))
        alpha = jnp.exp2(m_prev - m_new)
        m_ref[rows, :] = m_new
        yield
        p = jnp.exp2(s - m_new)
        l_ref[rows, :] = alpha * l_ref[rows, :] + jnp.sum(p, axis=-1, keepdims=True)
        yield
        pv = _dot(p.astype(BF16), v_ref[keys, :])
        yield
        a_ref[rows, :] = alpha * a_ref[rows, :] + pv
        yield

    def key_chunk(c, carry):
        keys = pl.ds(pl.multiple_of(c * tk, tk), tk)
        for r0 in range(0, ATTN_ROW_SPLIT, ATTN_GROUP):
            blocks = [block(keys, r, mp_i, *refs)
                      for r in range(r0, r0 + ATTN_GROUP) for mp_i, refs in enumerate(stats)]
            for _ in range(ATTN_STAGES):
                for b in blocks:
                    next(b)
        return carry

    lax.fori_loop(0, nk, key_chunk, 0, unroll=min(nk, ATTN_KEY_UNROLL))

    lp = lam_ref[...]
    lam = (jnp.exp(jnp.sum(lp[0:1] * lp[1:2], axis=-1, keepdims=True))
           - jnp.exp(jnp.sum(lp[2:3] * lp[3:4], axis=-1, keepdims=True)) + lam_init)
    o = a1_ref[...] / l1_ref[...] - lam * (a2_ref[...] / l2_ref[...])
    y = o * lax.rsqrt(jnp.mean(o * o, axis=-1, keepdims=True) + EPS) * ng_ref[...]
    o_ref[...] = (y * (1.0 - lam_init)).astype(o_ref.dtype)


def _attn_into_kernel(q_ref, k_ref, v_ref, lam_ref, ng_ref, _, o_ref, *scratch, **kw):
    _attn_kernel(q_ref, k_ref, v_ref, lam_ref, ng_ref, o_ref, *scratch, **kw)


def _attention(qr, kr, z, lam_p, ng, lam_init, q_row0, n_q_rows, kv_row0, n_kv_rows, tq, tk, into=None):
    nq = n_q_rows // tq
    nk = n_kv_rows // tk
    qb0 = q_row0 // tq
    kvb = kv_row0 // n_kv_rows
    assert kv_row0 % n_kv_rows == 0
    in_specs = [
        pl.BlockSpec((tq, C_DV), lambda h, i: (qb0 + i, h)),
        pl.BlockSpec((n_kv_rows, C_DV), lambda h, i: (kvb, h)),
        pl.BlockSpec((n_kv_rows, C_DV), lambda h, i: (kvb, Z_AV // C_DV + h)),
        pl.BlockSpec((4, C_DH), lambda h, i: (0, 0)),
        pl.BlockSpec((1, C_DV), lambda h, i: (0, 0)),
    ]
    args = (qr, kr, z, lam_p, ng)
    kern, aliases = _attn_kernel, {}
    if into is not None:
        in_specs.append(pl.BlockSpec(memory_space=pl.ANY))
        args += (into,)
        kern, aliases = _attn_into_kernel, {len(args) - 1: 0}
    return pl.pallas_call(
        functools.partial(kern, nk=nk, tk=tk, lam_init=lam_init),
        grid=(C_HEADS, nq),
        in_specs=in_specs,
        out_specs=pl.BlockSpec((tq, C_DV), lambda h, i: (qb0 + i, h)),
        out_shape=jax.ShapeDtypeStruct((qr.shape[0], C_HEADS * C_DV), BF16),
        input_output_aliases=aliases,
        scratch_shapes=[
            pltpu.VMEM((tq, 1), F32), pltpu.VMEM((tq, 1), F32), pltpu.VMEM((tq, C_DV), F32),
            pltpu.VMEM((tq, 1), F32), pltpu.VMEM((tq, 1), F32), pltpu.VMEM((tq, C_DV), F32),
        ],
        compiler_params=_params(("parallel", "arbitrary")),
        name="diff_attention",
    )(*args)


def _merge_kernel(ya_ref, yb_ref, yc_ref, g0_ref, g1_ref, g2_ref, wb_ref, y_ref, *, tm):
    def body(rows, _):
        y = _sigmoid(g0_ref[rows, :].astype(F32)) * _dot(ya_ref[rows, :], wb_ref[0])
        y += _sigmoid(g1_ref[rows, :].astype(F32)) * _dot(yb_ref[rows, :], wb_ref[1])
        y += _sigmoid(g2_ref[rows, :].astype(F32)) * _dot(yc_ref[rows, :], wb_ref[2])
        y_ref[rows, :] = y.astype(y_ref.dtype)

    _for_row_subtiles(tm, body)


def _merge(ya, yb, yc, z, w_branch, l):
    n = ya.shape[0]
    d = w_branch.shape[-1]
    tm = _row_tile(n, DENSE_ROWS)
    tn = min(d, 512)
    bw = BRANCH_WIDTH
    gate_blk = Z_GATE // tn
    row = lambda i, j: (i, 0)
    return pl.pallas_call(
        functools.partial(_merge_kernel, tm=tm),
        grid=(n // tm, d // tn),
        in_specs=[
            pl.BlockSpec((tm, bw), row),
            pl.BlockSpec((tm, bw), row),
            pl.BlockSpec((tm, bw), row),
            pl.BlockSpec((tm, tn), lambda i, j: (i, gate_blk + j)),
            pl.BlockSpec((tm, tn), lambda i, j: (i, gate_blk + d // tn + j)),
            pl.BlockSpec((tm, tn), lambda i, j: (i, gate_blk + 2 * (d // tn) + j)),
            pl.BlockSpec((None, N_BRANCH, bw, tn), lambda i, j: (l, 0, 0, j)),
        ],
        out_specs=pl.BlockSpec((tm, tn), lambda i, j: (i, j)),
        out_shape=jax.ShapeDtypeStruct((n, d), BF16),
        compiler_params=_params(("parallel", "arbitrary")),
        name="branch_merge",
    )(ya, yb, yc, z, z, z, w_branch)


def _final_kernel(x_ref, g_ref, o_ref):
    x = x_ref[...]
    o_ref[...] = x * lax.rsqrt(jnp.mean(x * x, axis=-1, keepdims=True) + EPS) * g_ref[...]


def _final_norm(xs, g, n_rows):
    d = xs.shape[1]
    tm = ELEM_ROWS
    return pl.pallas_call(
        _final_kernel,
        grid=(n_rows // tm,),
        in_specs=[pl.BlockSpec((tm, d), lambda i: (i, 0)), pl.BlockSpec((1, d), lambda i: (0, 0))],
        out_specs=pl.BlockSpec((tm, d), lambda i: (i, 0)),
        out_shape=jax.ShapeDtypeStruct((n_rows, d), F32),
        compiler_params=_params(("parallel",)),
        name="final_norm",
    )(xs, g)


def _rope_tables(n_lat, n_rows):
    t = jnp.arange(n_lat)
    r = (t // GRID_W).astype(F32)
    col = (t % GRID_W).astype(F32)
    n_freq = C_DH // 4
    inv = ROPE_THETA ** (-jnp.arange(n_freq, dtype=F32) / n_freq)
    ang = jnp.concatenate([r[:, None] * inv, col[:, None] * inv], axis=-1)
    cos = jnp.repeat(jnp.cos(ang), 2, axis=-1)
    sin = jnp.stack([-jnp.sin(ang), jnp.sin(ang)], axis=-1).reshape(n_lat, C_DH)
    pad = n_rows - n_lat
    cos = jnp.concatenate([cos, jnp.ones((pad, C_DH), F32)], axis=0)
    sin = jnp.concatenate([sin, jnp.zeros((pad, C_DH), F32)], axis=0)
    return cos, sin


def kernel(x, c, ctx, c_ctx, ada_down, ada_up, ada_bias, norm_g, ffn_w13, ffn_w2, w_in, gmlp_norm_g,
           gmlp_ws, gmlp_bs, mlstm_gate_b, mlstm_norm_g, diff_lambda, diff_norm_g, w_branch, w_out,
           final_g):
    bsz, seq, d = x.shape
    n_ctx = ctx.shape[1]
    depth = ada_down.shape[0]
    n_tok = seq + n_ctx
    assert bsz == 1 and seq % ELEM_ROWS == 0 and n_ctx % ELEM_ROWS == 0
    assert w_in.shape[-1] == Z_GATE + GATE_COLS + N_BRANCH * d

    xs = jnp.concatenate([x[0], ctx[0]], axis=0)
    w2 = ffn_w2.astype(BF16)
    w_in_t = jnp.swapaxes(w_in, 1, 2)
    wbr = w_branch.astype(BF16)
    wo = w_out.astype(BF16)
    cos_t, sin_t = _rope_tables(seq, n_tok)
    gate_b = jnp.pad(mlstm_gate_b.reshape(depth, 1, GATE_COLS), ((0, 0), (0, 0), (0, LANES - GATE_COLS)))
    cv = jnp.concatenate([c, c_ctx[None], jnp.zeros((6, d), F32)], axis=0)

    mod = _modulation(cv, ada_down, ada_up, ada_bias)[:, :2].reshape(depth, 2, N_SUB, 3, d)

    tq = 1024 if seq % 1024 == 0 else ELEM_ROWS
    tk = 1408 if n_tok % 1408 == 0 else ELEM_ROWS
    for l in range(depth):
        lam_init = 0.8 - 0.6 * math.exp(-0.3 * l)

        def ffn(xs, s, sub):
            h = _normmod(xs, norm_g[l, sub:sub + 1], mod[l, :, sub], seq)
            act = _ffn_up(h, ffn_w13, l, s)
            return _resid_proj(act, w2, (l, s), xs, mod[l, :, sub, 2:3], seq, 0.5)

        xs = ffn(xs, 0, 0)

        h = _normmod(xs, norm_g[l, 1:2], mod[l, :, 1], seq)
        z, zg = _proj(h, w_in_t, gate_b[l], l)
        ya = _gmlp(z, n_tok, gmlp_norm_g[l][None], gmlp_ws[l], gmlp_bs[l].T)
        hf, hb = _mlstm(z, zg, seq // CHUNK, n_ctx // CHUNK)
        yb = _mlstm_out(hf, hb, z, mlstm_norm_g[l][None])
        qr, kr = _rope(z, cos_t, sin_t)
        ng = diff_norm_g[l][None]
        yc = _attention(qr, kr, z, diff_lambda[l], ng, lam_init, 0, seq, 0, n_tok, tq, tk)
        yc = _attention(qr, kr, z, diff_lambda[l], ng, lam_init, seq, n_ctx, seq, n_ctx, n_ctx, n_ctx, into=yc)
        y = _merge(ya, yb, yc, z, wbr, l)
        xs = _resid_proj(y, wo, (l,), xs, mod[l, :, 1, 2:3], seq, 1.0)

        xs = ffn(xs, 1, 2)
    return _final_norm(xs, final_g[None], seq)[None]
```

```python
import functools
import math

import jax
import jax.numpy as jnp
from jax import lax
from jax.experimental import pallas as pl
from jax.experimental.pallas import tpu as pltpu

F32 = jnp.float32
BF16 = jnp.bfloat16

GRID_W = 64
CHUNK = 128
BRANCH_WIDTH = 1024
A_GROUPS = 8
A_GDIM = BRANCH_WIDTH // A_GROUPS
B_HEADS = 4
B_DQK = 128
B_DV = BRANCH_WIDTH // B_HEADS
C_HEADS = 4
C_DH = 128
C_DV = 2 * C_DH
N_BRANCH = 3
N_SUB = 3
ROPE_THETA = 10000.0
EPS = 1e-6
GATE_COLS = 4 * B_HEADS
LANES = 128
Z_UV, Z_MQ, Z_MK, Z_MV, Z_MO, Z_AQ, Z_AK, Z_AV, Z_GATE = (
    0, 2048, 2560, 3072, 4096, 5120, 6144, 7168, 8192)
DENSE_ROWS = 1408
ROW_SPLIT = 2
DENSE_COLS = 512
PROJ_COLS = 1024
HEAD_COLS = 256
ELEM_ROWS = 256
ATTN_Q_ROWS = 1024
ATTN_KEY_ROWS = 1408
ATTN_KEY_UNROLL = 3
ATTN_ROW_SPLIT = 4
ATTN_GROUP = 2
ATTN_STAGES = 5
MLSTM_STAGES = 5
VMEM_LIMIT = 56 * 1024 * 1024
LOG2E = math.log2(math.e)


def _params(sem):
    return pltpu.CompilerParams(dimension_semantics=sem, vmem_limit_bytes=VMEM_LIMIT)


def _dot(a, b):
    return jnp.dot(a, b, preferred_element_type=F32)


def _dot_nt(a, b):
    return lax.dot_general(a, b, (((1,), (1,)), ((), ())), preferred_element_type=F32)


def _sigmoid(x):
    return 0.5 * jnp.tanh(0.5 * x) + 0.5


def _log_sigmoid(x):
    return jnp.minimum(x, 0.0) - jnp.log(1.0 + jnp.exp(-jnp.abs(x)))


def _row_tile(n, cap):
    unit = 16 * ROW_SPLIT
    return max(t for t in range(unit, min(n, cap) + 1, unit) if n % t == 0)


def _for_row_subtiles(tm, body):
    sub = tm // ROW_SPLIT

    def step(r, carry):
        body(pl.ds(pl.multiple_of(r * sub, sub), sub), r * sub)
        return carry

    lax.fori_loop(0, ROW_SPLIT, step, 0)


def _is_ctx(row0, n, seq):
    return row0 + lax.broadcasted_iota(jnp.int32, (n, 1), 0) >= seq


def _mod_kernel(cv_ref, down_ref, up_ref, b_ref, o_ref, t_ref):
    @pl.when(pl.program_id(1) == 0)
    def _():
        s = cv_ref[...]
        s = s * _sigmoid(s)
        t_ref[...] = _dot(s.astype(BF16), down_ref[...].astype(BF16))

    o_ref[...] = _dot(t_ref[...].astype(BF16), up_ref[...].astype(BF16)) + b_ref[...]


def _modulation(cv, ada_down, ada_up, ada_bias):
    depth, d, r = ada_down.shape
    n = ada_up.shape[-1]
    tn = 3 * d
    return pl.pallas_call(
        _mod_kernel,
        grid=(depth, n // tn),
        in_specs=[
            pl.BlockSpec((8, d), lambda l, j: (0, 0)),
            pl.BlockSpec((None, d, r), lambda l, j: (l, 0, 0)),
            pl.BlockSpec((None, r, tn), lambda l, j: (l, 0, j)),
            pl.BlockSpec((None, 1, tn), lambda l, j: (l, 0, j)),
        ],
        out_specs=pl.BlockSpec((None, 8, tn), lambda l, j: (l, 0, j)),
        out_shape=jax.ShapeDtypeStruct((depth, 8, n), F32),
        scratch_shapes=[pltpu.VMEM((8, r), F32)],
        compiler_params=_params(("parallel", "arbitrary")),
        name="adaln_modulation",
    )(cv, ada_down, ada_up, ada_bias.reshape(depth, 1, n))


def _normmod_kernel(x_ref, g_ref, mod_ref, h_ref, *, seq, tm):
    stream = (pl.program_id(0) * tm >= seq).astype(jnp.int32)
    m = mod_ref[stream]
    gain = g_ref[...] * (1.0 + m[1:2, :])
    x = x_ref[...]
    y = x * lax.rsqrt(jnp.mean(x * x, axis=-1, keepdims=True) + EPS)
    h_ref[...] = (y * gain + m[0:1, :]).astype(h_ref.dtype)


def _normmod(xs, g, mod, seq):
    n, d = xs.shape
    tm = ELEM_ROWS
    assert seq % tm == 0 and n % tm == 0
    return pl.pallas_call(
        functools.partial(_normmod_kernel, seq=seq, tm=tm),
        grid=(n // tm,),
        in_specs=[
            pl.BlockSpec((tm, d), lambda i: (i, 0)),
            pl.BlockSpec((1, d), lambda i: (0, 0)),
            pl.BlockSpec((2, 3, d), lambda i: (0, 0, 0)),
        ],
        out_specs=pl.BlockSpec((tm, d), lambda i: (i, 0)),
        out_shape=jax.ShapeDtypeStruct((n, d), BF16),
        compiler_params=_params(("parallel",)),
        name="norm_modulate",
    )(xs, g, mod)


def _ffn_up_head_kernel(h_ref, wa_ref, wb_ref, o_ref, wa_out, wb_out, *, tm):
    wa_out[...] = wa_ref[...].astype(BF16)
    wb_out[...] = wb_ref[...].astype(BF16)
    _ffn_up_rows(h_ref, wa_out, wb_out, o_ref, tm)


def _ffn_up_tail_kernel(h_ref, wa_ref, wb_ref, head_ref, o_ref, *, tm):
    i = pl.program_id(0)

    @pl.when(i == 0)
    def _():
        o_ref[...] = head_ref[...]

    @pl.when(i > 0)
    def _():
        _ffn_up_rows(h_ref, wa_ref, wb_ref, o_ref, tm)


def _ffn_up_rows(h_ref, wa_ref, wb_ref, o_ref, tm):
    def body(rows, _):
        h = h_ref[rows, :]
        a = _dot(h, wa_ref[...])
        b = _dot(h, wb_ref[...])
        o_ref[rows, :] = (a * _sigmoid(a) * b).astype(o_ref.dtype)

    _for_row_subtiles(tm, body)


def _ffn_up(h, w13, l, s):
    n, d = h.shape
    f = w13.shape[-1] // 2
    tm = _row_tile(n, DENSE_ROWS)
    th = min(f, HEAD_COLS)
    nh = f // th
    wcopy = jax.ShapeDtypeStruct((d, f), BF16)
    act, wa, wb = pl.pallas_call(
        functools.partial(_ffn_up_head_kernel, tm=tm),
        grid=(nh,),
        in_specs=[
            pl.BlockSpec((tm, d), lambda j: (0, 0), pipeline_mode=pl.Buffered(1)),
            pl.BlockSpec((None, None, d, th), lambda j: (l, s, 0, j)),
            pl.BlockSpec((None, None, d, th), lambda j: (l, s, 0, nh + j)),
        ],
        out_specs=[
            pl.BlockSpec((tm, th), lambda j: (0, j)),
            pl.BlockSpec((d, th), lambda j: (0, j)),
            pl.BlockSpec((d, th), lambda j: (0, j)),
        ],
        out_shape=[jax.ShapeDtypeStruct((tm, f), BF16), wcopy, wcopy],
        compiler_params=_params(("arbitrary",)),
        name="ffn_up_head",
    )(h, w13, w13)
    if n == tm:
        return act
    tf = min(f, DENSE_COLS)
    nj = f // tf
    wcol = lambda i, j: (0, jnp.where(i == 0, 0, j))
    return pl.pallas_call(
        functools.partial(_ffn_up_tail_kernel, tm=tm),
        grid=(n // tm, nj),
        in_specs=[
            pl.BlockSpec((tm, d), lambda i, j: (jnp.maximum(i, 1), 0)),
            pl.BlockSpec((d, tf), wcol),
            pl.BlockSpec((d, tf), wcol),
            pl.BlockSpec((tm, tf), lambda i, j: (0, jnp.where(i == 0, j, nj - 1))),
        ],
        out_specs=pl.BlockSpec((tm, tf), lambda i, j: (i, j)),
        out_shape=jax.ShapeDtypeStruct((n, f), BF16),
        compiler_params=_params(("arbitrary", "arbitrary")),
        name="ffn_up_tail",
    )(h, wa, wb, act)


def _resid_kernel(a_ref, w_ref, x_ref, gate_ref, o_ref, *, tm, seq, coef):
    row_base = pl.program_id(0) * tm

    def body(rows, r0):
        sub = tm // ROW_SPLIT
        gate = jnp.where(_is_ctx(row_base + r0, sub, seq), gate_ref[1], gate_ref[0])
        o_ref[rows, :] = x_ref[rows, :] + (coef * gate) * _dot(a_ref[rows, :], w_ref[...])

    _for_row_subtiles(tm, body)


def _resid_proj(a, w, widx, xs, gate, seq, coef):
    n, d = xs.shape
    k = a.shape[1]
    tm = _row_tile(n, DENSE_ROWS)
    tn = min(d, DENSE_COLS)
    lead = (None,) * len(widx)
    return pl.pallas_call(
        functools.partial(_resid_kernel, tm=tm, seq=seq, coef=coef),
        grid=(n // tm, d // tn),
        in_specs=[
            pl.BlockSpec((tm, k), lambda i, j: (i, 0)),
            pl.BlockSpec(lead + (k, tn), lambda i, j: widx + (0, j)),
            pl.BlockSpec((tm, tn), lambda i, j: (i, j)),
            pl.BlockSpec((2, 1, tn), lambda i, j: (0, 0, j)),
        ],
        out_specs=pl.BlockSpec((tm, tn), lambda i, j: (i, j)),
        out_shape=jax.ShapeDtypeStruct((n, d), F32),
        compiler_params=_params(("parallel", "arbitrary")),
        name="gated_residual_proj",
    )(a, w, xs, gate)


def _proj_head_kernel(h_ref, wt_ref, wgt_ref, gb_ref, z_ref, zg_ref, wt_out, wgt_out, *, tm):
    wt_out[...] = wt_ref[0].astype(BF16)

    @pl.when(pl.program_id(0) == 0)
    def _():
        wgt_out[...] = wgt_ref[0].astype(BF16)

    _proj_rows(h_ref, wt_out, wgt_out, gb_ref, z_ref, zg_ref, tm, pl.program_id(0) == 0)


def _proj_tail_kernel(h_ref, wt_ref, wgt_ref, gb_ref, zh_ref, zgh_ref, z_ref, zg_ref, *, tm):
    i = pl.program_id(0)
    first_col_step = pl.program_id(1) == 0

    @pl.when(i == 0)
    def _():
        z_ref[...] = zh_ref[...]

        @pl.when(first_col_step)
        def _():
            zg_ref[...] = zgh_ref[...]

    @pl.when(i > 0)
    def _():
        _proj_rows(h_ref, wt_ref, wgt_ref, gb_ref, z_ref, zg_ref, tm, first_col_step)


def _proj_rows(h_ref, wt_ref, wgt_ref, gb_ref, z_ref, zg_ref, tm, first_col_step):
    def body(rows, _):
        h = h_ref[rows, :]
        z_ref[rows, :] = _dot_nt(h, wt_ref[...]).astype(z_ref.dtype)

        @pl.when(first_col_step)
        def _():
            zg_ref[rows, :] = _dot_nt(h, wgt_ref[...]) + gb_ref[...]

    _for_row_subtiles(tm, body)


def _proj(h, w_t, gate_b, l):
    n, d = h.shape
    g0 = Z_MO + BRANCH_WIDTH
    nz = w_t.shape[1] - GATE_COLS
    tm = _row_tile(n, DENSE_ROWS)
    th = 2 * HEAD_COLS
    assert g0 % th == 0 and nz % th == 0
    outs = [jax.ShapeDtypeStruct((tm, nz), BF16), jax.ShapeDtypeStruct((tm, LANES), F32)]
    z, zg, wt, wgt = pl.pallas_call(
        functools.partial(_proj_head_kernel, tm=tm),
        grid=(nz // th,),
        in_specs=[
            pl.BlockSpec((tm, d), lambda j: (0, 0), pipeline_mode=pl.Buffered(1)),
            pl.BlockSpec((pl.Element(1), pl.Element(th), pl.Element(d)),
                         lambda j: (l, pl.multiple_of(
                             j * th + jnp.where(j * th >= g0, GATE_COLS, 0), GATE_COLS), 0)),
            pl.BlockSpec((pl.Element(1), pl.Element(LANES), pl.Element(d)), lambda j: (l, g0, 0)),
            pl.BlockSpec((1, LANES), lambda j: (0, 0)),
        ],
        out_specs=[
            pl.BlockSpec((tm, th), lambda j: (0, j)),
            pl.BlockSpec((tm, LANES), lambda j: (0, 0)),
            pl.BlockSpec((th, d), lambda j: (j, 0)),
            pl.BlockSpec((LANES, d), lambda j: (0, 0)),
        ],
        out_shape=outs + [jax.ShapeDtypeStruct((nz, d), BF16), jax.ShapeDtypeStruct((LANES, d), BF16)],
        compiler_params=_params(("arbitrary",)),
        name="mixer_in_proj_head",
    )(h, w_t, w_t, gate_b)
    if n == tm:
        return z, zg
    tn = PROJ_COLS
    nj = nz // tn
    return pl.pallas_call(
        functools.partial(_proj_tail_kernel, tm=tm),
        grid=(n // tm, nj),
        in_specs=[
            pl.BlockSpec((tm, d), lambda i, j: (jnp.maximum(i, 1), 0)),
            pl.BlockSpec((tn, d), lambda i, j: (jnp.where(i == 0, 0, j), 0)),
            pl.BlockSpec((LANES, d), lambda i, j: (0, 0)),
            pl.BlockSpec((1, LANES), lambda i, j: (0, 0)),
            pl.BlockSpec((tm, tn), lambda i, j: (0, jnp.where(i == 0, j, nj - 1)),
                         pipeline_mode=pl.Buffered(1)),
            pl.BlockSpec((tm, LANES), lambda i, j: (0, 0), pipeline_mode=pl.Buffered(1)),
        ],
        out_specs=[
            pl.BlockSpec((tm, tn), lambda i, j: (i, j)),
            pl.BlockSpec((tm, LANES), lambda i, j: (i, 0)),
        ],
        out_shape=[jax.ShapeDtypeStruct((n, nz), BF16), jax.ShapeDtypeStruct((n, LANES), F32)],
        compiler_params=_params(("arbitrary", "arbitrary")),
        name="mixer_in_proj_tail",
    )(h, wt, wgt, gate_b, z, zg)


def _gmlp_kernel(u_ref, v_ref, ng_ref, ws_ref, bst_ref, o_ref, *, tm):
    v = jax.nn.gelu(v_ref[...].astype(F32))
    vc = v - jnp.mean(v, axis=-1, keepdims=True)
    vn = vc * lax.rsqrt(jnp.mean(vc * vc, axis=-1, keepdims=True) + EPS) * ng_ref[...]
    vn = vn.astype(BF16)
    for c in range(tm // CHUNK):
        rows = slice(c * CHUNK, (c + 1) * CHUNK)
        for g in range(A_GROUPS):
            cols = slice(g * A_GDIM, (g + 1) * A_GDIM)
            mixed = _dot(ws_ref[g].astype(BF16), vn[rows, cols]) + bst_ref[:, g:g + 1]
            u = jax.nn.gelu(u_ref[rows, cols].astype(F32))
            o_ref[rows, cols] = (u * mixed).astype(o_ref.dtype)


def _gmlp(z, n, ng, ws, bst):
    tm = ELEM_ROWS
    bw = BRANCH_WIDTH
    return pl.pallas_call(
        functools.partial(_gmlp_kernel, tm=tm),
        grid=(n // tm,),
        in_specs=[
            pl.BlockSpec((tm, bw), lambda i: (i, Z_UV // bw)),
            pl.BlockSpec((tm, bw), lambda i: (i, Z_UV // bw + 1)),
            pl.BlockSpec((1, bw), lambda i: (0, 0)),
            pl.BlockSpec((A_GROUPS, CHUNK, CHUNK), lambda i: (0, 0, 0)),
            pl.BlockSpec((CHUNK, A_GROUPS), lambda i: (0, 0)),
        ],
        out_specs=pl.BlockSpec((tm, bw), lambda i: (i, 0)),
        out_shape=jax.ShapeDtypeStruct((n, bw), BF16),
        compiler_params=_params(("parallel",)),
        name="gmlp_branch",
    )(z, z, ng, ws, bst)


def _mlstm_head(rev, q, k, v, gi_col, gf_col, gi_row, gf_row, c_state, n_state, m_state):
    L = q.shape[0]
    r = lax.broadcasted_iota(jnp.int32, (L, L), 0)
    c = lax.broadcasted_iota(jnp.int32, (L, L), 1)
    seen = (c >= r) if rev else (c <= r)
    seen_t = (r >= c) if rev else (r <= c)
    lf_col = _log_sigmoid(gf_col)
    lf_row = _log_sigmoid(gf_row)
    b_col = jnp.sum(jnp.where(seen, lf_row, 0.0), axis=1, keepdims=True)
    b_row = jnp.sum(jnp.where(seen_t, lf_col, 0.0), axis=0, keepdims=True)
    b_last = jnp.sum(lf_row, axis=1, keepdims=True)
    qk = _dot_nt(q, k)
    yield

    dmat = jnp.where(seen, b_col - b_row + gi_row, -jnp.inf)
    a_col = b_col + m_state
    m_t = jnp.maximum(a_col, jnp.max(dmat, axis=1, keepdims=True))
    g_row = b_last - b_row + gi_row
    g_col = b_last - b_col + gi_col
    m_new = jnp.maximum(b_last + m_state, jnp.max(g_row, axis=1, keepdims=True))
    yield

    wa = jnp.exp(a_col - m_t)
    s = qk * jnp.exp(dmat - m_t)
    decay = jnp.exp(b_last + m_state - m_new)
    kw = k.astype(F32) * jnp.exp(g_col - m_new)
    yield

    sv = _dot(s.astype(BF16), v)
    qc = _dot(q, c_state.astype(BF16))
    kv = lax.dot_general(kw.astype(BF16), v, (((0,), (0,)), ((), ())), preferred_element_type=F32)
    yield

    qn = jnp.sum(q.astype(F32) * n_state, axis=1, keepdims=True)
    den = wa * qn + jnp.sum(s, axis=1, keepdims=True)
    h = (wa * qc + sv) / jnp.maximum(jnp.abs(den), jnp.exp(-m_t))
    c_new = decay * c_state + kv
    n_new = decay * n_state + jnp.sum(kw, axis=0, keepdims=True)
    yield h, c_new, n_new, m_new


def _mlstm_kernel(qf_ref, kf_ref, vf_ref, gf_ref, qb_ref, kb_ref, vb_ref, gb_ref,
                  hf_ref, hb_ref, c_ref, n_ref, m_ref):
    @pl.when(pl.program_id(0) == 0)
    def _():
        c_ref[...] = jnp.zeros_like(c_ref)
        n_ref[...] = jnp.zeros_like(n_ref)
        m_ref[...] = jnp.zeros_like(m_ref)

    heads = []
    for d, (q_ref, k_ref, v_ref, g_ref, h_ref) in enumerate(
            ((qf_ref, kf_ref, vf_ref, gf_ref, hf_ref), (qb_ref, kb_ref, vb_ref, gb_ref, hb_ref))):
        gates = g_ref[...]
        gates_t = gates.T
        for hd in range(B_HEADS):
            idx = d * B_HEADS + hd
            ci = d * 2 * B_HEADS + hd
            cf = ci + B_HEADS
            qs = slice(hd * B_DQK, (hd + 1) * B_DQK)
            vs = slice(hd * B_DV, (hd + 1) * B_DV)
            q = (q_ref[:, qs].astype(F32) * (B_DQK ** -0.5)).astype(BF16)
            stages = _mlstm_head(
                d == 1, q, k_ref[:, qs], v_ref[:, vs],
                gates[:, ci:ci + 1], gates[:, cf:cf + 1],
                gates_t[ci:ci + 1, :], gates_t[cf:cf + 1, :],
                c_ref[idx], n_ref[idx], m_ref[idx][:, 0:1])
            heads.append((stages, idx, h_ref, vs))

    for _ in range(MLSTM_STAGES - 1):
        for stages, _, _, _ in heads:
            next(stages)
    for stages, idx, h_ref, vs in heads:
        h, c_new, n_new, m_new = next(stages)
        h_ref[:, vs] = h
        c_ref[idx] = c_new
        n_ref[idx] = n_new
        m_ref[idx] = jnp.broadcast_to(m_new, (1, LANES))


def _mlstm(z, zg, n_lat, n_ctx):
    n_steps = n_lat + n_ctx
    n = n_steps * CHUNK

    def fwd(t):
        return (t + n_lat) % n_steps

    def bwd(t):
        return n_steps - 1 - t

    def specs(order):
        return [
            pl.BlockSpec((CHUNK, B_HEADS * B_DQK), lambda t: (order(t), Z_MQ // (B_HEADS * B_DQK))),
            pl.BlockSpec((CHUNK, B_HEADS * B_DQK), lambda t: (order(t), Z_MK // (B_HEADS * B_DQK))),
            pl.BlockSpec((CHUNK, BRANCH_WIDTH), lambda t: (order(t), Z_MV // BRANCH_WIDTH)),
            pl.BlockSpec((CHUNK, LANES), lambda t: (order(t), 0)),
        ]

    return pl.pallas_call(
        _mlstm_kernel,
        grid=(n_steps,),
        in_specs=specs(fwd) + specs(bwd),
        out_specs=[
            pl.BlockSpec((CHUNK, BRANCH_WIDTH), lambda t: (fwd(t), 0)),
            pl.BlockSpec((CHUNK, BRANCH_WIDTH), lambda t: (bwd(t), 0)),
        ],
        out_shape=[jax.ShapeDtypeStruct((n, BRANCH_WIDTH), F32)] * 2,
        scratch_shapes=[
            pltpu.VMEM((2 * B_HEADS, B_DQK, B_DV), F32),
            pltpu.VMEM((2 * B_HEADS, 1, B_DQK), F32),
            pltpu.VMEM((2 * B_HEADS, 1, LANES), F32),
        ],
        compiler_params=_params(("arbitrary",)),
        name="mlstm_bidir",
    )(z, z, z, zg, z, z, z, zg)


def _mlstm_out_kernel(hf_ref, hb_ref, og_ref, g_ref, o_ref):
    for hd in range(B_HEADS):
        cols = slice(hd * B_DV, (hd + 1) * B_DV)
        h = hf_ref[:, cols] + hb_ref[:, cols]
        hn = h * lax.rsqrt(jnp.mean(h * h, axis=-1, keepdims=True) + EPS) * g_ref[:, cols]
        o_ref[:, cols] = (_sigmoid(og_ref[:, cols].astype(F32)) * hn).astype(o_ref.dtype)


def _mlstm_out(hf, hb, z, g):
    n = hf.shape[0]
    tm = ELEM_ROWS
    bw = BRANCH_WIDTH
    row = lambda i: (i, 0)
    return pl.pallas_call(
        _mlstm_out_kernel,
        grid=(n // tm,),
        in_specs=[
            pl.BlockSpec((tm, bw), row),
            pl.BlockSpec((tm, bw), row),
            pl.BlockSpec((tm, bw), lambda i: (i, Z_MO // bw)),
            pl.BlockSpec((1, bw), lambda i: (0, 0)),
        ],
        out_specs=pl.BlockSpec((tm, bw), row),
        out_shape=jax.ShapeDtypeStruct((n, bw), BF16),
        compiler_params=_params(("parallel",)),
        name="mlstm_out_gate",
    )(hf, hb, z, g)


def _rope_kernel(q_ref, k_ref, cos_ref, sin_ref, qo_ref, ko_ref):
    cos = cos_ref[...]
    sin = sin_ref[...]
    even = lax.broadcasted_iota(jnp.int32, cos.shape, 1) % 2 == 0
    for src, dst, mult in ((q_ref, qo_ref, C_DH ** -0.5 * LOG2E), (k_ref, ko_ref, None)):
        for g in range(2 * C_HEADS):
            cols = slice(g * C_DH, (g + 1) * C_DH)
            x = src[:, cols].astype(F32)
            partner = jnp.where(even, pltpu.roll(x, C_DH - 1, 1), pltpu.roll(x, 1, 1))
            y = x * cos + partner * sin
            if mult is not None:
                y = y * mult
            dst[:, cols] = y.astype(dst.dtype)


def _rope(z, cos_t, sin_t):
    n = cos_t.shape[0]
    tm = ELEM_ROWS
    w = 2 * C_HEADS * C_DH
    return pl.pallas_call(
        _rope_kernel,
        grid=(n // tm,),
        in_specs=[
            pl.BlockSpec((tm, w), lambda i: (i, Z_AQ // w)),
            pl.BlockSpec((tm, w), lambda i: (i, Z_AK // w)),
            pl.BlockSpec((tm, C_DH), lambda i: (i, 0)),
            pl.BlockSpec((tm, C_DH), lambda i: (i, 0)),
        ],
        out_specs=[pl.BlockSpec((tm, w), lambda i: (i, 0))] * 2,
        out_shape=[jax.ShapeDtypeStruct((n, w), BF16)] * 2,
        compiler_params=_params(("parallel",)),
        name="axial_rope",
    )(z, z, cos_t, sin_t)


def _attn_kernel(q_ref, k_ref, v_ref, lam_ref, ng_ref, o_ref,
                 m1_ref, l1_ref, a1_ref, m2_ref, l2_ref, a2_ref, *, nk, tk, lam_init):
    stats = ((m1_ref, l1_ref, a1_ref), (m2_ref, l2_ref, a2_ref))
    for m_ref, l_ref, a_ref in stats:
        m_ref[...] = jnp.full_like(m_ref, -jnp.inf)
        l_ref[...] = jnp.zeros_like(l_ref)
        a_ref[...] = jnp.zeros_like(a_ref)

    sub = q_ref.shape[0] // ATTN_ROW_SPLIT

    def block(keys, r, mp_i, m_ref, l_ref, a_ref):
        rows = slice(r * sub, (r + 1) * sub)
        cols = slice(mp_i * C_DH, (mp_i + 1) * C_DH)
        s = _dot_nt(q_ref[rows, cols], k_ref[keys, cols])
        yield
        m_prev = m_ref[rows, :]
        m_new = jnp.maximum(m_prev, jnp.max(s, axis=-1, keepdims=True))
        alpha = jnp.exp2(m_prev - m_new)
        m_ref[rows, :] = m_new
        yield
        p = jnp.exp2(s - m_new)
        l_ref[rows, :] = alpha * l_ref[rows, :] + jnp.sum(p, axis=-1, keepdims=True)
        yield
        pv = _dot(p.astype(BF16), v_ref[keys, :])
        yield
        a_ref[rows, :] = alpha * a_ref[rows, :] + pv
        yield

    def key_chunk(c, carry):
        keys = pl.ds(pl.multiple_of(c * tk, tk), tk)
        for r0 in range(0, ATTN_ROW_SPLIT, ATTN_GROUP):
            blocks = [block(keys, r, mp_i, *refs)
                      for r in range(r0, r0 + ATTN_GROUP) for mp_i, refs in enumerate(stats)]
            for _ in range(ATTN_STAGES):
                for b in blocks:
                    next(b)
        return carry

    lax.fori_loop(0, nk, key_chunk, 0, unroll=min(nk, ATTN_KEY_UNROLL))

    lp = lam_ref[...]
    lam = (jnp.exp(jnp.sum(lp[0:1] * lp[1:2], axis=-1, keepdims=True))
           - jnp.exp(jnp.sum(lp[2:3] * lp[3:4], axis=-1, keepdims=True)) + lam_init)
    o = a1_ref[...] / l1_ref[...] - lam * (a2_ref[...] / l2_ref[...])
    y = o * lax.rsqrt(jnp.mean(o * o, axis=-1, keepdims=True) + EPS) * ng_ref[...]
    o_ref[...] = (y * (1.0 - lam_init)).astype(o_ref.dtype)


def _attention(qr, kr, z, lam_p, ng, lam_init, q_row0, n_q_rows, kv_row0, n_kv_rows, tq, tk):
    nq = n_q_rows // tq
    nk = n_kv_rows // tk
    qb0 = q_row0 // tq
    kvb = kv_row0 // n_kv_rows
    assert kv_row0 % n_kv_rows == 0
    return pl.pallas_call(
        functools.partial(_attn_kernel, nk=nk, tk=tk, lam_init=lam_init),
        grid=(C_HEADS, nq),
        in_specs=[
            pl.BlockSpec((tq, C_DV), lambda h, i: (qb0 + i, h)),
            pl.BlockSpec((n_kv_rows, C_DV), lambda h, i: (kvb, h)),
            pl.BlockSpec((n_kv_rows, C_DV), lambda h, i: (kvb, Z_AV // C_DV + h)),
            pl.BlockSpec((4, C_DH), lambda h, i: (0, 0)),
            pl.BlockSpec((1, C_DV), lambda h, i: (0, 0)),
        ],
        out_specs=pl.BlockSpec((tq, C_DV), lambda h, i: (i, h)),
        out_shape=jax.ShapeDtypeStruct((n_q_rows, C_HEADS * C_DV), BF16),
        scratch_shapes=[
            pltpu.VMEM((tq, 1), F32), pltpu.VMEM((tq, 1), F32), pltpu.VMEM((tq, C_DV), F32),
            pltpu.VMEM((tq, 1), F32), pltpu.VMEM((tq, 1), F32), pltpu.VMEM((tq, C_DV), F32),
        ],
        compiler_params=_params(("parallel", "arbitrary")),
        name="diff_attention",
    )(qr, kr, z, lam_p, ng)


def _merge_kernel(ya_ref, yb_ref, yc_ref, g0_ref, g1_ref, g2_ref, wb_ref, y_ref, *, tm):
    def body(rows, _):
        y = _sigmoid(g0_ref[rows, :].astype(F32)) * _dot(ya_ref[rows, :], wb_ref[0])
        y += _sigmoid(g1_ref[rows, :].astype(F32)) * _dot(yb_ref[rows, :], wb_ref[1])
        y += _sigmoid(g2_ref[rows, :].astype(F32)) * _dot(yc_ref[rows, :], wb_ref[2])
        y_ref[rows, :] = y.astype(y_ref.dtype)

    _for_row_subtiles(tm, body)


def _merge(ya, yb, yc, z, w_branch, l):
    n = ya.shape[0]
    d = w_branch.shape[-1]
    tm = _row_tile(n, DENSE_ROWS)
    tn = min(d, DENSE_COLS)
    bw = BRANCH_WIDTH
    gate_blk = Z_GATE // tn
    row = lambda i, j: (i, 0)
    return pl.pallas_call(
        functools.partial(_merge_kernel, tm=tm),
        grid=(n // tm, d // tn),
        in_specs=[
            pl.BlockSpec((tm, bw), row),
            pl.BlockSpec((tm, bw), row),
            pl.BlockSpec((tm, bw), row),
            pl.BlockSpec((tm, tn), lambda i, j: (i, gate_blk + j)),
            pl.BlockSpec((tm, tn), lambda i, j: (i, gate_blk + d // tn + j)),
            pl.BlockSpec((tm, tn), lambda i, j: (i, gate_blk + 2 * (d // tn) + j)),
            pl.BlockSpec((None, N_BRANCH, bw, tn), lambda i, j: (l, 0, 0, j)),
        ],
        out_specs=pl.BlockSpec((tm, tn), lambda i, j: (i, j)),
        out_shape=jax.ShapeDtypeStruct((n, d), BF16),
        compiler_params=_params(("parallel", "arbitrary")),
        name="branch_merge",
    )(ya, yb, yc, z, z, z, w_branch)


def _final_kernel(x_ref, g_ref, o_ref):
    x = x_ref[...]
    o_ref[...] = x * lax.rsqrt(jnp.mean(x * x, axis=-1, keepdims=True) + EPS) * g_ref[...]


def _final_norm(xs, g, n_rows):
    d = xs.shape[1]
    tm = ELEM_ROWS
    return pl.pallas_call(
        _final_kernel,
        grid=(n_rows // tm,),
        in_specs=[pl.BlockSpec((tm, d), lambda i: (i, 0)), pl.BlockSpec((1, d), lambda i: (0, 0))],
        out_specs=pl.BlockSpec((tm, d), lambda i: (i, 0)),
        out_shape=jax.ShapeDtypeStruct((n_rows, d), F32),
        compiler_params=_params(("parallel",)),
        name="final_norm",
    )(xs, g)


def _rope_tables(n_lat, n_rows):
    t = jnp.arange(n_lat)
    r = (t // GRID_W).astype(F32)
    col = (t % GRID_W).astype(F32)
    n_freq = C_DH // 4
    inv = ROPE_THETA ** (-jnp.arange(n_freq, dtype=F32) / n_freq)
    ang = jnp.concatenate([r[:, None] * inv, col[:, None] * inv], axis=-1)
    cos = jnp.repeat(jnp.cos(ang), 2, axis=-1)
    sin = jnp.stack([-jnp.sin(ang), jnp.sin(ang)], axis=-1).reshape(n_lat, C_DH)
    pad = n_rows - n_lat
    cos = jnp.concatenate([cos, jnp.ones((pad, C_DH), F32)], axis=0)
    sin = jnp.concatenate([sin, jnp.zeros((pad, C_DH), F32)], axis=0)
    return cos, sin


def kernel(x, c, ctx, c_ctx, ada_down, ada_up, ada_bias, norm_g, ffn_w13, ffn_w2, w_in, gmlp_norm_g,
           gmlp_ws, gmlp_bs, mlstm_gate_b, mlstm_norm_g, diff_lambda, diff_norm_g, w_branch, w_out,
           final_g):
    bsz, seq, d = x.shape
    n_ctx = ctx.shape[1]
    depth = ada_down.shape[0]
    n_tok = seq + n_ctx
    assert bsz == 1 and seq % ELEM_ROWS == 0 and n_ctx % ELEM_ROWS == 0
    assert w_in.shape[-1] == Z_GATE + GATE_COLS + N_BRANCH * d

    xs = jnp.concatenate([x[0], ctx[0]], axis=0)
    w2 = ffn_w2.astype(BF16)
    w_in_t = jnp.swapaxes(w_in, 1, 2)
    wbr = w_branch.astype(BF16)
    wo = w_out.astype(BF16)
    cos_t, sin_t = _rope_tables(seq, n_tok)
    gate_b = jnp.pad(mlstm_gate_b.reshape(depth, 1, GATE_COLS), ((0, 0), (0, 0), (0, LANES - GATE_COLS)))
    cv = jnp.concatenate([c, c_ctx[None], jnp.zeros((6, d), F32)], axis=0)

    mod = _modulation(cv, ada_down, ada_up, ada_bias)[:, :2].reshape(depth, 2, N_SUB, 3, d)

    tq = ATTN_Q_ROWS if seq % ATTN_Q_ROWS == 0 else ELEM_ROWS
    tk = ATTN_KEY_ROWS if n_tok % ATTN_KEY_ROWS == 0 else ELEM_ROWS
    for l in range(depth):
        lam_init = 0.8 - 0.6 * math.exp(-0.3 * l)

        def ffn(xs, s, sub):
            h = _normmod(xs, norm_g[l, sub:sub + 1], mod[l, :, sub], seq)
            act = _ffn_up(h, ffn_w13, l, s)
            return _resid_proj(act, w2, (l, s), xs, mod[l, :, sub, 2:3], seq, 0.5)

        xs = ffn(xs, 0, 0)

        h = _normmod(xs, norm_g[l, 1:2], mod[l, :, 1], seq)
        z, zg = _proj(h, w_in_t, gate_b[l], l)
        ya = _gmlp(z, n_tok, gmlp_norm_g[l][None], gmlp_ws[l], gmlp_bs[l].T)
        hf, hb = _mlstm(z, zg, seq // CHUNK, n_ctx // CHUNK)
        yb = _mlstm_out(hf, hb, z, mlstm_norm_g[l][None])
        qr, kr = _rope(z, cos_t, sin_t)
        ng = diff_norm_g[l][None]
        yc_lat = _attention(qr, kr, z, diff_lambda[l], ng, lam_init, 0, seq, 0, n_tok, tq, tk)
        yc_ctx = _attention(qr, kr, z, diff_lambda[l], ng, lam_init, seq, n_ctx, seq, n_ctx, n_ctx, n_ctx)
        yc = jnp.concatenate([yc_lat, yc_ctx], axis=0)
        y = _merge(ya, yb, yc, z, wbr, l)
        xs = _resid_proj(y, wo, (l,), xs, mod[l, :, 1, 2:3], seq, 1.0)

        xs = ffn(xs, 1, 2)
    return _final_norm(xs, final_g[None], seq)[None]
```

```python
import functools
import math

import jax
import jax.numpy as jnp
from jax import lax
from jax.experimental import pallas as pl
from jax.experimental.pallas import tpu as pltpu

F32 = jnp.float32
BF16 = jnp.bfloat16

GRID_W = 64
CHUNK = 128
BRANCH_WIDTH = 1024
A_GROUPS = 8
A_GDIM = BRANCH_WIDTH // A_GROUPS
B_HEADS = 4
B_DQK = 128
B_DV = BRANCH_WIDTH // B_HEADS
C_HEADS = 4
C_DH = 128
C_DV = 2 * C_DH
N_BRANCH = 3
N_SUB = 3
ROPE_THETA = 10000.0
EPS = 1e-6
GATE_COLS = 4 * B_HEADS
LANES = 128
Z_UV, Z_MQ, Z_MK, Z_MV, Z_MO, Z_AQ, Z_AK, Z_AV, Z_GATE = (
    0, 2048, 2560, 3072, 4096, 5120, 6144, 7168, 8192)
DENSE_ROWS = 1408
ROW_SPLIT = 2
DENSE_COLS = 512
PROJ_COLS = 1024
HEAD_COLS = 256
ELEM_ROWS = 256
ATTN_Q_ROWS = 1024
ATTN_KEY_ROWS = 2816
ATTN_KEY_UNROLL = 1
ATTN_ROW_SPLIT = 4
ATTN_GROUP = 2
ATTN_STAGES = 5
MLSTM_STAGES = 5
VMEM_LIMIT = 56 * 1024 * 1024
LOG2E = math.log2(math.e)


def _params(sem):
    return pltpu.CompilerParams(dimension_semantics=sem, vmem_limit_bytes=VMEM_LIMIT)


def _dot(a, b):
    return jnp.dot(a, b, preferred_element_type=F32)


def _dot_nt(a, b):
    return lax.dot_general(a, b, (((1,), (1,)), ((), ())), preferred_element_type=F32)


def _sigmoid(x):
    return 0.5 * jnp.tanh(0.5 * x) + 0.5


def _log_sigmoid(x):
    return jnp.minimum(x, 0.0) - jnp.log(1.0 + jnp.exp(-jnp.abs(x)))


def _row_tile(n, cap):
    unit = 16 * ROW_SPLIT
    return max(t for t in range(unit, min(n, cap) + 1, unit) if n % t == 0)


def _for_row_subtiles(tm, body):
    sub = tm // ROW_SPLIT

    def step(r, carry):
        body(pl.ds(pl.multiple_of(r * sub, sub), sub), r * sub)
        return carry

    lax.fori_loop(0, ROW_SPLIT, step, 0)


def _is_ctx(row0, n, seq):
    return row0 + lax.broadcasted_iota(jnp.int32, (n, 1), 0) >= seq


def _mod_kernel(cv_ref, down_ref, up_ref, b_ref, o_ref, t_ref):
    @pl.when(pl.program_id(1) == 0)
    def _():
        s = cv_ref[...]
        s = s * _sigmoid(s)
        t_ref[...] = _dot(s.astype(BF16), down_ref[...].astype(BF16))

    o_ref[...] = _dot(t_ref[...].astype(BF16), up_ref[...].astype(BF16)) + b_ref[...]


def _modulation(cv, ada_down, ada_up, ada_bias):
    depth, d, r = ada_down.shape
    n = ada_up.shape[-1]
    tn = 3 * d
    return pl.pallas_call(
        _mod_kernel,
        grid=(depth, n // tn),
        in_specs=[
            pl.BlockSpec((8, d), lambda l, j: (0, 0)),
            pl.BlockSpec((None, d, r), lambda l, j: (l, 0, 0)),
            pl.BlockSpec((None, r, tn), lambda l, j: (l, 0, j)),
            pl.BlockSpec((None, 1, tn), lambda l, j: (l, 0, j)),
        ],
        out_specs=pl.BlockSpec((None, 8, tn), lambda l, j: (l, 0, j)),
        out_shape=jax.ShapeDtypeStruct((depth, 8, n), F32),
        scratch_shapes=[pltpu.VMEM((8, r), F32)],
        compiler_params=_params(("parallel", "arbitrary")),
        name="adaln_modulation",
    )(cv, ada_down, ada_up, ada_bias.reshape(depth, 1, n))


def _normmod_kernel(x_ref, g_ref, mod_ref, h_ref, *, seq, tm):
    stream = (pl.program_id(0) * tm >= seq).astype(jnp.int32)
    m = mod_ref[stream]
    gain = g_ref[...] * (1.0 + m[1:2, :])
    x = x_ref[...]
    y = x * lax.rsqrt(jnp.mean(x * x, axis=-1, keepdims=True) + EPS)
    h_ref[...] = (y * gain + m[0:1, :]).astype(h_ref.dtype)


def _normmod(xs, g, mod, seq):
    n, d = xs.shape
    tm = ELEM_ROWS
    assert seq % tm == 0 and n % tm == 0
    return pl.pallas_call(
        functools.partial(_normmod_kernel, seq=seq, tm=tm),
        grid=(n // tm,),
        in_specs=[
            pl.BlockSpec((tm, d), lambda i: (i, 0)),
            pl.BlockSpec((1, d), lambda i: (0, 0)),
            pl.BlockSpec((2, 3, d), lambda i: (0, 0, 0)),
        ],
        out_specs=pl.BlockSpec((tm, d), lambda i: (i, 0)),
        out_shape=jax.ShapeDtypeStruct((n, d), BF16),
        compiler_params=_params(("parallel",)),
        name="norm_modulate",
    )(xs, g, mod)


def _ffn_up_head_kernel(h_ref, wa_ref, wb_ref, o_ref, wa_out, wb_out, *, tm):
    wa_out[...] = wa_ref[...].astype(BF16)
    wb_out[...] = wb_ref[...].astype(BF16)
    _ffn_up_rows(h_ref, wa_out, wb_out, o_ref, tm)


def _ffn_up_tail_kernel(h_ref, wa_ref, wb_ref, head_ref, o_ref, *, tm):
    i = pl.program_id(0)

    @pl.when(i == 0)
    def _():
        o_ref[...] = head_ref[...]

    @pl.when(i > 0)
    def _():
        _ffn_up_rows(h_ref, wa_ref, wb_ref, o_ref, tm)


def _ffn_up_rows(h_ref, wa_ref, wb_ref, o_ref, tm):
    def body(rows, _):
        h = h_ref[rows, :]
        a = _dot(h, wa_ref[...])
        b = _dot(h, wb_ref[...])
        o_ref[rows, :] = (a * _sigmoid(a) * b).astype(o_ref.dtype)

    _for_row_subtiles(tm, body)


def _ffn_up(h, w13, l, s):
    n, d = h.shape
    f = w13.shape[-1] // 2
    tm = _row_tile(n, DENSE_ROWS)
    th = min(f, HEAD_COLS)
    nh = f // th
    wcopy = jax.ShapeDtypeStruct((d, f), BF16)
    act, wa, wb = pl.pallas_call(
        functools.partial(_ffn_up_head_kernel, tm=tm),
        grid=(nh,),
        in_specs=[
            pl.BlockSpec((tm, d), lambda j: (0, 0), pipeline_mode=pl.Buffered(1)),
            pl.BlockSpec((None, None, d, th), lambda j: (l, s, 0, j)),
            pl.BlockSpec((None, None, d, th), lambda j: (l, s, 0, nh + j)),
        ],
        out_specs=[
            pl.BlockSpec((tm, th), lambda j: (0, j)),
            pl.BlockSpec((d, th), lambda j: (0, j)),
            pl.BlockSpec((d, th), lambda j: (0, j)),
        ],
        out_shape=[jax.ShapeDtypeStruct((tm, f), BF16), wcopy, wcopy],
        compiler_params=_params(("arbitrary",)),
        name="ffn_up_head",
    )(h, w13, w13)
    if n == tm:
        return act
    tf = min(f, DENSE_COLS)
    nj = f // tf
    wcol = lambda i, j: (0, jnp.where(i == 0, 0, j))
    return pl.pallas_call(
        functools.partial(_ffn_up_tail_kernel, tm=tm),
        grid=(n // tm, nj),
        in_specs=[
            pl.BlockSpec((tm, d), lambda i, j: (jnp.maximum(i, 1), 0)),
            pl.BlockSpec((d, tf), wcol),
            pl.BlockSpec((d, tf), wcol),
            pl.BlockSpec((tm, tf), lambda i, j: (0, jnp.where(i == 0, j, nj - 1))),
        ],
        out_specs=pl.BlockSpec((tm, tf), lambda i, j: (i, j)),
        out_shape=jax.ShapeDtypeStruct((n, f), BF16),
        compiler_params=_params(("arbitrary", "arbitrary")),
        name="ffn_up_tail",
    )(h, wa, wb, act)


def _resid_kernel(a_ref, w_ref, x_ref, gate_ref, o_ref, *, tm, seq, coef):
    row_base = pl.program_id(0) * tm

    def body(rows, r0):
        sub = tm // ROW_SPLIT
        gate = jnp.where(_is_ctx(row_base + r0, sub, seq), gate_ref[1], gate_ref[0])
        o_ref[rows, :] = x_ref[rows, :] + (coef * gate) * _dot(a_ref[rows, :], w_ref[...])

    _for_row_subtiles(tm, body)


def _resid_proj(a, w, widx, xs, gate, seq, coef):
    n, d = xs.shape
    k = a.shape[1]
    tm = _row_tile(n, DENSE_ROWS)
    tn = min(d, DENSE_COLS)
    lead = (None,) * len(widx)
    return pl.pallas_call(
        functools.partial(_resid_kernel, tm=tm, seq=seq, coef=coef),
        grid=(n // tm, d // tn),
        in_specs=[
            pl.BlockSpec((tm, k), lambda i, j: (i, 0)),
            pl.BlockSpec(lead + (k, tn), lambda i, j: widx + (0, j)),
            pl.BlockSpec((tm, tn), lambda i, j: (i, j)),
            pl.BlockSpec((2, 1, tn), lambda i, j: (0, 0, j)),
        ],
        out_specs=pl.BlockSpec((tm, tn), lambda i, j: (i, j)),
        out_shape=jax.ShapeDtypeStruct((n, d), F32),
        compiler_params=_params(("parallel", "arbitrary")),
        name="gated_residual_proj",
    )(a, w, xs, gate)


def _proj_head_kernel(h_ref, wt_ref, wgt_ref, gb_ref, z_ref, zg_ref, wt_out, wgt_out, *, tm):
    wt_out[...] = wt_ref[0].astype(BF16)

    @pl.when(pl.program_id(0) == 0)
    def _():
        wgt_out[...] = wgt_ref[0].astype(BF16)

    _proj_rows(h_ref, wt_out, wgt_out, gb_ref, z_ref, zg_ref, tm, pl.program_id(0) == 0)


def _proj_tail_kernel(h_ref, wt_ref, wgt_ref, gb_ref, zh_ref, zgh_ref, z_ref, zg_ref, *, tm):
    i = pl.program_id(0)
    first_col_step = pl.program_id(1) == 0

    @pl.when(i == 0)
    def _():
        z_ref[...] = zh_ref[...]

        @pl.when(first_col_step)
        def _():
            zg_ref[...] = zgh_ref[...]

    @pl.when(i > 0)
    def _():
        _proj_rows(h_ref, wt_ref, wgt_ref, gb_ref, z_ref, zg_ref, tm, first_col_step)


def _proj_rows(h_ref, wt_ref, wgt_ref, gb_ref, z_ref, zg_ref, tm, first_col_step):
    def body(rows, _):
        h = h_ref[rows, :]
        z_ref[rows, :] = _dot_nt(h, wt_ref[...]).astype(z_ref.dtype)

        @pl.when(first_col_step)
        def _():
            zg_ref[rows, :] = _dot_nt(h, wgt_ref[...]) + gb_ref[...]

    _for_row_subtiles(tm, body)


def _proj(h, w_t, gate_b, l):
    n, d = h.shape
    g0 = Z_MO + BRANCH_WIDTH
    nz = w_t.shape[1] - GATE_COLS
    tm = _row_tile(n, DENSE_ROWS)
    th = 2 * HEAD_COLS
    assert g0 % th == 0 and nz % th == 0
    outs = [jax.ShapeDtypeStruct((tm, nz), BF16), jax.ShapeDtypeStruct((tm, LANES), F32)]
    z, zg, wt, wgt = pl.pallas_call(
        functools.partial(_proj_head_kernel, tm=tm),
        grid=(nz // th,),
        in_specs=[
            pl.BlockSpec((tm, d), lambda j: (0, 0), pipeline_mode=pl.Buffered(1)),
            pl.BlockSpec((pl.Element(1), pl.Element(th), pl.Element(d)),
                         lambda j: (l, pl.multiple_of(
                             j * th + jnp.where(j * th >= g0, GATE_COLS, 0), GATE_COLS), 0)),
            pl.BlockSpec((pl.Element(1), pl.Element(LANES), pl.Element(d)), lambda j: (l, g0, 0)),
            pl.BlockSpec((1, LANES), lambda j: (0, 0)),
        ],
        out_specs=[
            pl.BlockSpec((tm, th), lambda j: (0, j)),
            pl.BlockSpec((tm, LANES), lambda j: (0, 0)),
            pl.BlockSpec((th, d), lambda j: (j, 0)),
            pl.BlockSpec((LANES, d), lambda j: (0, 0)),
        ],
        out_shape=outs + [jax.ShapeDtypeStruct((nz, d), BF16), jax.ShapeDtypeStruct((LANES, d), BF16)],
        compiler_params=_params(("arbitrary",)),
        name="mixer_in_proj_head",
    )(h, w_t, w_t, gate_b)
    if n == tm:
        return z, zg
    tn = PROJ_COLS
    nj = nz // tn
    return pl.pallas_call(
        functools.partial(_proj_tail_kernel, tm=tm),
        grid=(n // tm, nj),
        in_specs=[
            pl.BlockSpec((tm, d), lambda i, j: (jnp.maximum(i, 1), 0)),
            pl.BlockSpec((tn, d), lambda i, j: (jnp.where(i == 0, 0, j), 0)),
            pl.BlockSpec((LANES, d), lambda i, j: (0, 0)),
            pl.BlockSpec((1, LANES), lambda i, j: (0, 0)),
            pl.BlockSpec((tm, tn), lambda i, j: (0, jnp.where(i == 0, j, nj - 1)),
                         pipeline_mode=pl.Buffered(1)),
            pl.BlockSpec((tm, LANES), lambda i, j: (0, 0), pipeline_mode=pl.Buffered(1)),
        ],
        out_specs=[
            pl.BlockSpec((tm, tn), lambda i, j: (i, j)),
            pl.BlockSpec((tm, LANES), lambda i, j: (i, 0)),
        ],
        out_shape=[jax.ShapeDtypeStruct((n, nz), BF16), jax.ShapeDtypeStruct((n, LANES), F32)],
        compiler_params=_params(("arbitrary", "arbitrary")),
        name="mixer_in_proj_tail",
    )(h, wt, wgt, gate_b, z, zg)


def _gmlp_kernel(u_ref, v_ref, ng_ref, ws_ref, bst_ref, o_ref, *, tm):
    v = jax.nn.gelu(v_ref[...].astype(F32))
    vc = v - jnp.mean(v, axis=-1, keepdims=True)
    vn = vc * lax.rsqrt(jnp.mean(vc * vc, axis=-1, keepdims=True) + EPS) * ng_ref[...]
    vn = vn.astype(BF16)
    for c in range(tm // CHUNK):
        rows = slice(c * CHUNK, (c + 1) * CHUNK)
        for g in range(A_GROUPS):
            cols = slice(g * A_GDIM, (g + 1) * A_GDIM)
            mixed = _dot(ws_ref[g].astype(BF16), vn[rows, cols]) + bst_ref[:, g:g + 1]
            u = jax.nn.gelu(u_ref[rows, cols].astype(F32))
            o_ref[rows, cols] = (u * mixed).astype(o_ref.dtype)


def _gmlp(z, n, ng, ws, bst):
    tm = ELEM_ROWS
    bw = BRANCH_WIDTH
    return pl.pallas_call(
        functools.partial(_gmlp_kernel, tm=tm),
        grid=(n // tm,),
        in_specs=[
            pl.BlockSpec((tm, bw), lambda i: (i, Z_UV // bw)),
            pl.BlockSpec((tm, bw), lambda i: (i, Z_UV // bw + 1)),
            pl.BlockSpec((1, bw), lambda i: (0, 0)),
            pl.BlockSpec((A_GROUPS, CHUNK, CHUNK), lambda i: (0, 0, 0)),
            pl.BlockSpec((CHUNK, A_GROUPS), lambda i: (0, 0)),
        ],
        out_specs=pl.BlockSpec((tm, bw), lambda i: (i, 0)),
        out_shape=jax.ShapeDtypeStruct((n, bw), BF16),
        compiler_params=_params(("parallel",)),
        name="gmlp_branch",
    )(z, z, ng, ws, bst)


def _mlstm_head(rev, q, k, v, gi_col, gf_col, gi_row, gf_row, c_state, n_state, m_state):
    L = q.shape[0]
    r = lax.broadcasted_iota(jnp.int32, (L, L), 0)
    c = lax.broadcasted_iota(jnp.int32, (L, L), 1)
    seen = (c >= r) if rev else (c <= r)
    seen_t = (r >= c) if rev else (r <= c)
    lf_col = _log_sigmoid(gf_col)
    lf_row = _log_sigmoid(gf_row)
    b_col = jnp.sum(jnp.where(seen, lf_row, 0.0), axis=1, keepdims=True)
    b_row = jnp.sum(jnp.where(seen_t, lf_col, 0.0), axis=0, keepdims=True)
    b_last = jnp.sum(lf_row, axis=1, keepdims=True)
    qk = _dot_nt(q, k)
    yield

    dmat = jnp.where(seen, b_col - b_row + gi_row, -jnp.inf)
    a_col = b_col + m_state
    m_t = jnp.maximum(a_col, jnp.max(dmat, axis=1, keepdims=True))
    g_row = b_last - b_row + gi_row
    g_col = b_last - b_col + gi_col
    m_new = jnp.maximum(b_last + m_state, jnp.max(g_row, axis=1, keepdims=True))
    yield

    wa = jnp.exp(a_col - m_t)
    s = qk * jnp.exp(dmat - m_t)
    decay = jnp.exp(b_last + m_state - m_new)
    kw = k.astype(F32) * jnp.exp(g_col - m_new)
    yield

    sv = _dot(s.astype(BF16), v)
    qc = _dot(q, c_state.astype(BF16))
    kv = lax.dot_general(kw.astype(BF16), v, (((0,), (0,)), ((), ())), preferred_element_type=F32)
    yield

    qn = jnp.sum(q.astype(F32) * n_state, axis=1, keepdims=True)
    den = wa * qn + jnp.sum(s, axis=1, keepdims=True)
    h = (wa * qc + sv) / jnp.maximum(jnp.abs(den), jnp.exp(-m_t))
    c_new = decay * c_state + kv
    n_new = decay * n_state + jnp.sum(kw, axis=0, keepdims=True)
    yield h, c_new, n_new, m_new


def _mlstm_kernel(qf_ref, kf_ref, vf_ref, gf_ref, qb_ref, kb_ref, vb_ref, gb_ref,
                  hf_ref, hb_ref, c_ref, n_ref, m_ref):
    @pl.when(pl.program_id(0) == 0)
    def _():
        c_ref[...] = jnp.zeros_like(c_ref)
        n_ref[...] = jnp.zeros_like(n_ref)
        m_ref[...] = jnp.zeros_like(m_ref)

    heads = []
    for d, (q_ref, k_ref, v_ref, g_ref, h_ref) in enumerate(
            ((qf_ref, kf_ref, vf_ref, gf_ref, hf_ref), (qb_ref, kb_ref, vb_ref, gb_ref, hb_ref))):
        gates = g_ref[...]
        gates_t = gates.T
        for hd in range(B_HEADS):
            idx = d * B_HEADS + hd
            ci = d * 2 * B_HEADS + hd
            cf = ci + B_HEADS
            qs = slice(hd * B_DQK, (hd + 1) * B_DQK)
            vs = slice(hd * B_DV, (hd + 1) * B_DV)
            q = (q_ref[:, qs].astype(F32) * (B_DQK ** -0.5)).astype(BF16)
            stages = _mlstm_head(
                d == 1, q, k_ref[:, qs], v_ref[:, vs],
                gates[:, ci:ci + 1], gates[:, cf:cf + 1],
                gates_t[ci:ci + 1, :], gates_t[cf:cf + 1, :],
                c_ref[idx], n_ref[idx], m_ref[idx][:, 0:1])
            heads.append((stages, idx, h_ref, vs))

    for _ in range(MLSTM_STAGES - 1):
        for stages, _, _, _ in heads:
            next(stages)
    for stages, idx, h_ref, vs in heads:
        h, c_new, n_new, m_new = next(stages)
        h_ref[:, vs] = h
        c_ref[idx] = c_new
        n_ref[idx] = n_new
        m_ref[idx] = jnp.broadcast_to(m_new, (1, LANES))


def _mlstm(z, zg, n_lat, n_ctx):
    n_steps = n_lat + n_ctx
    n = n_steps * CHUNK

    def fwd(t):
        return (t + n_lat) % n_steps

    def bwd(t):
        return n_steps - 1 - t

    def specs(order):
        return [
            pl.BlockSpec((CHUNK, B_HEADS * B_DQK), lambda t: (order(t), Z_MQ // (B_HEADS * B_DQK))),
            pl.BlockSpec((CHUNK, B_HEADS * B_DQK), lambda t: (order(t), Z_MK // (B_HEADS * B_DQK))),
            pl.BlockSpec((CHUNK, BRANCH_WIDTH), lambda t: (order(t), Z_MV // BRANCH_WIDTH)),
            pl.BlockSpec((CHUNK, LANES), lambda t: (order(t), 0)),
        ]

    return pl.pallas_call(
        _mlstm_kernel,
        grid=(n_steps,),
        in_specs=specs(fwd) + specs(bwd),
        out_specs=[
            pl.BlockSpec((CHUNK, BRANCH_WIDTH), lambda t: (fwd(t), 0)),
            pl.BlockSpec((CHUNK, BRANCH_WIDTH), lambda t: (bwd(t), 0)),
        ],
        out_shape=[jax.ShapeDtypeStruct((n, BRANCH_WIDTH), F32)] * 2,
        scratch_shapes=[
            pltpu.VMEM((2 * B_HEADS, B_DQK, B_DV), F32),
            pltpu.VMEM((2 * B_HEADS, 1, B_DQK), F32),
            pltpu.VMEM((2 * B_HEADS, 1, LANES), F32),
        ],
        compiler_params=_params(("arbitrary",)),
        name="mlstm_bidir",
    )(z, z, z, zg, z, z, z, zg)


def _mlstm_out_kernel(hf_ref, hb_ref, og_ref, g_ref, o_ref):
    for hd in range(B_HEADS):
        cols = slice(hd * B_DV, (hd + 1) * B_DV)
        h = hf_ref[:, cols] + hb_ref[:, cols]
        hn = h * lax.rsqrt(jnp.mean(h * h, axis=-1, keepdims=True) + EPS) * g_ref[:, cols]
        o_ref[:, cols] = (_sigmoid(og_ref[:, cols].astype(F32)) * hn).astype(o_ref.dtype)


def _mlstm_out(hf, hb, z, g):
    n = hf.shape[0]
    tm = ELEM_ROWS
    bw = BRANCH_WIDTH
    row = lambda i: (i, 0)
    return pl.pallas_call(
        _mlstm_out_kernel,
        grid=(n // tm,),
        in_specs=[
            pl.BlockSpec((tm, bw), row),
            pl.BlockSpec((tm, bw), row),
            pl.BlockSpec((tm, bw), lambda i: (i, Z_MO // bw)),
            pl.BlockSpec((1, bw), lambda i: (0, 0)),
        ],
        out_specs=pl.BlockSpec((tm, bw), row),
        out_shape=jax.ShapeDtypeStruct((n, bw), BF16),
        compiler_params=_params(("parallel",)),
        name="mlstm_out_gate",
    )(hf, hb, z, g)


def _rope_kernel(q_ref, k_ref, cos_ref, sin_ref, qo_ref, ko_ref):
    cos = cos_ref[...]
    sin = sin_ref[...]
    even = lax.broadcasted_iota(jnp.int32, cos.shape, 1) % 2 == 0
    for src, dst, mult in ((q_ref, qo_ref, C_DH ** -0.5 * LOG2E), (k_ref, ko_ref, None)):
        for g in range(2 * C_HEADS):
            cols = slice(g * C_DH, (g + 1) * C_DH)
            x = src[:, cols].astype(F32)
            partner = jnp.where(even, pltpu.roll(x, C_DH - 1, 1), pltpu.roll(x, 1, 1))
            y = x * cos + partner * sin
            if mult is not None:
                y = y * mult
            dst[:, cols] = y.astype(dst.dtype)


def _rope(z, cos_t, sin_t):
    n = cos_t.shape[0]
    tm = ELEM_ROWS
    w = 2 * C_HEADS * C_DH
    return pl.pallas_call(
        _rope_kernel,
        grid=(n // tm,),
        in_specs=[
            pl.BlockSpec((tm, w), lambda i: (i, Z_AQ // w)),
            pl.BlockSpec((tm, w), lambda i: (i, Z_AK // w)),
            pl.BlockSpec((tm, C_DH), lambda i: (i, 0)),
            pl.BlockSpec((tm, C_DH), lambda i: (i, 0)),
        ],
        out_specs=[pl.BlockSpec((tm, w), lambda i: (i, 0))] * 2,
        out_shape=[jax.ShapeDtypeStruct((n, w), BF16)] * 2,
        compiler_params=_params(("parallel",)),
        name="axial_rope",
    )(z, z, cos_t, sin_t)


def _attn_kernel(q_ref, k_ref, v_ref, lam_ref, ng_ref, o_ref,
                 m1_ref, l1_ref, a1_ref, m2_ref, l2_ref, a2_ref, *, nk, tk, lam_init):
    stats = ((m1_ref, l1_ref, a1_ref), (m2_ref, l2_ref, a2_ref))
    for m_ref, l_ref, a_ref in stats:
        m_ref[...] = jnp.full_like(m_ref, -jnp.inf)
        l_ref[...] = jnp.zeros_like(l_ref)
        a_ref[...] = jnp.zeros_like(a_ref)

    sub = q_ref.shape[0] // ATTN_ROW_SPLIT

    def block(keys, r, mp_i, m_ref, l_ref, a_ref):
        rows = slice(r * sub, (r + 1) * sub)
        cols = slice(mp_i * C_DH, (mp_i + 1) * C_DH)
        s = _dot_nt(q_ref[rows, cols], k_ref[keys, cols])
        yield
        m_prev = m_ref[rows, :]
        m_new = jnp.maximum(m_prev, jnp.max(s, axis=-1, keepdims=True))
        alpha = jnp.exp2(m_prev - m_new)
        m_ref[rows, :] = m_new
        yield
        p = jnp.exp2(s - m_new)
        l_ref[rows, :] = alpha * l_ref[rows, :] + jnp.sum(p, axis=-1, keepdims=True)
        yield
        pv = _dot(p.astype(BF16), v_ref[keys, :])
        yield
        a_ref[rows, :] = alpha * a_ref[rows, :] + pv
        yield

    def key_chunk(c, carry):
        keys = pl.ds(pl.multiple_of(c * tk, tk), tk)
        for r0 in range(0, ATTN_ROW_SPLIT, ATTN_GROUP):
            blocks = [block(keys, r, mp_i, *refs)
                      for r in range(r0, r0 + ATTN_GROUP) for mp_i, refs in enumerate(stats)]
            for _ in range(ATTN_STAGES):
                for b in blocks:
                    next(b)
        return carry

    lax.fori_loop(0, nk, key_chunk, 0, unroll=min(nk, ATTN_KEY_UNROLL))

    lp = lam_ref[...]
    lam = (jnp.exp(jnp.sum(lp[0:1] * lp[1:2], axis=-1, keepdims=True))
           - jnp.exp(jnp.sum(lp[2:3] * lp[3:4], axis=-1, keepdims=True)) + lam_init)
    o = a1_ref[...] / l1_ref[...] - lam * (a2_ref[...] / l2_ref[...])
    y = o * lax.rsqrt(jnp.mean(o * o, axis=-1, keepdims=True) + EPS) * ng_ref[...]
    o_ref[...] = (y * (1.0 - lam_init)).astype(o_ref.dtype)


def _attention(qr, kr, z, lam_p, ng, lam_init, q_row0, n_q_rows, kv_row0, n_kv_rows, tq, tk):
    nq = n_q_rows // tq
    nk = n_kv_rows // tk
    qb0 = q_row0 // tq
    kvb = kv_row0 // n_kv_rows
    assert kv_row0 % n_kv_rows == 0
    return pl.pallas_call(
        functools.partial(_attn_kernel, nk=nk, tk=tk, lam_init=lam_init),
        grid=(C_HEADS, nq),
        in_specs=[
            pl.BlockSpec((tq, C_DV), lambda h, i: (qb0 + i, h)),
            pl.BlockSpec((n_kv_rows, C_DV), lambda h, i: (kvb, h)),
            pl.BlockSpec((n_kv_rows, C_DV), lambda h, i: (kvb, Z_AV // C_DV + h)),
            pl.BlockSpec((4, C_DH), lambda h, i: (0, 0)),
            pl.BlockSpec((1, C_DV), lambda h, i: (0, 0)),
        ],
        out_specs=pl.BlockSpec((tq, C_DV), lambda h, i: (i, h)),
        out_shape=jax.ShapeDtypeStruct((n_q_rows, C_HEADS * C_DV), BF16),
        scratch_shapes=[
            pltpu.VMEM((tq, 1), F32), pltpu.VMEM((tq, 1), F32), pltpu.VMEM((tq, C_DV), F32),
            pltpu.VMEM((tq, 1), F32), pltpu.VMEM((tq, 1), F32), pltpu.VMEM((tq, C_DV), F32),
        ],
        compiler_params=_params(("parallel", "arbitrary")),
        name="diff_attention",
    )(qr, kr, z, lam_p, ng)


def _merge_kernel(ya_ref, yb_ref, yc_ref, g0_ref, g1_ref, g2_ref, wb_ref, y_ref, *, tm):
    def body(rows, _):
        y = _sigmoid(g0_ref[rows, :].astype(F32)) * _dot(ya_ref[rows, :], wb_ref[0])
        y += _sigmoid(g1_ref[rows, :].astype(F32)) * _dot(yb_ref[rows, :], wb_ref[1])
        y += _sigmoid(g2_ref[rows, :].astype(F32)) * _dot(yc_ref[rows, :], wb_ref[2])
        y_ref[rows, :] = y.astype(y_ref.dtype)

    _for_row_subtiles(tm, body)


def _merge(ya, yb, yc, z, w_branch, l):
    n = ya.shape[0]
    d = w_branch.shape[-1]
    tm = _row_tile(n, DENSE_ROWS)
    tn = min(d, DENSE_COLS)
    bw = BRANCH_WIDTH
    gate_blk = Z_GATE // tn
    row = lambda i, j: (i, 0)
    return pl.pallas_call(
        functools.partial(_merge_kernel, tm=tm),
        grid=(n // tm, d // tn),
        in_specs=[
            pl.BlockSpec((tm, bw), row),
            pl.BlockSpec((tm, bw), row),
            pl.BlockSpec((tm, bw), row),
            pl.BlockSpec((tm, tn), lambda i, j: (i, gate_blk + j)),
            pl.BlockSpec((tm, tn), lambda i, j: (i, gate_blk + d // tn + j)),
            pl.BlockSpec((tm, tn), lambda i, j: (i, gate_blk + 2 * (d // tn) + j)),
            pl.BlockSpec((None, N_BRANCH, bw, tn), lambda i, j: (l, 0, 0, j)),
        ],
        out_specs=pl.BlockSpec((tm, tn), lambda i, j: (i, j)),
        out_shape=jax.ShapeDtypeStruct((n, d), BF16),
        compiler_params=_params(("parallel", "arbitrary")),
        name="branch_merge",
    )(ya, yb, yc, z, z, z, w_branch)


def _final_kernel(x_ref, g_ref, o_ref):
    x = x_ref[...]
    o_ref[...] = x * lax.rsqrt(jnp.mean(x * x, axis=-1, keepdims=True) + EPS) * g_ref[...]


def _final_norm(xs, g, n_rows):
    d = xs.shape[1]
    tm = ELEM_ROWS
    return pl.pallas_call(
        _final_kernel,
        grid=(n_rows // tm,),
        in_specs=[pl.BlockSpec((tm, d), lambda i: (i, 0)), pl.BlockSpec((1, d), lambda i: (0, 0))],
        out_specs=pl.BlockSpec((tm, d), lambda i: (i, 0)),
        out_shape=jax.ShapeDtypeStruct((n_rows, d), F32),
        compiler_params=_params(("parallel",)),
        name="final_norm",
    )(xs, g)


def _rope_tables(n_lat, n_rows):
    t = jnp.arange(n_lat)
    r = (t // GRID_W).astype(F32)
    col = (t % GRID_W).astype(F32)
    n_freq = C_DH // 4
    inv = ROPE_THETA ** (-jnp.arange(n_freq, dtype=F32) / n_freq)
    ang = jnp.concatenate([r[:, None] * inv, col[:, None] * inv], axis=-1)
    cos = jnp.repeat(jnp.cos(ang), 2, axis=-1)
    sin = jnp.stack([-jnp.sin(ang), jnp.sin(ang)], axis=-1).reshape(n_lat, C_DH)
    pad = n_rows - n_lat
    cos = jnp.concatenate([cos, jnp.ones((pad, C_DH), F32)], axis=0)
    sin = jnp.concatenate([sin, jnp.zeros((pad, C_DH), F32)], axis=0)
    return cos, sin


def kernel(x, c, ctx, c_ctx, ada_down, ada_up, ada_bias, norm_g, ffn_w13, ffn_w2, w_in, gmlp_norm_g,
           gmlp_ws, gmlp_bs, mlstm_gate_b, mlstm_norm_g, diff_lambda, diff_norm_g, w_branch, w_out,
           final_g):
    bsz, seq, d = x.shape
    n_ctx = ctx.shape[1]
    depth = ada_down.shape[0]
    n_tok = seq + n_ctx
    assert bsz == 1 and seq % ELEM_ROWS == 0 and n_ctx % ELEM_ROWS == 0
    assert w_in.shape[-1] == Z_GATE + GATE_COLS + N_BRANCH * d

    xs = jnp.concatenate([x[0], ctx[0]], axis=0)
    w2 = ffn_w2.astype(BF16)
    w_in_t = jnp.swapaxes(w_in, 1, 2)
    wbr = w_branch.astype(BF16)
    wo = w_out.astype(BF16)
    cos_t, sin_t = _rope_tables(seq, n_tok)
    gate_b = jnp.pad(mlstm_gate_b.reshape(depth, 1, GATE_COLS), ((0, 0), (0, 0), (0, LANES - GATE_COLS)))
    cv = jnp.concatenate([c, c_ctx[None], jnp.zeros((6, d), F32)], axis=0)

    mod = _modulation(cv, ada_down, ada_up, ada_bias)[:, :2].reshape(depth, 2, N_SUB, 3, d)

    tq = ATTN_Q_ROWS if seq % ATTN_Q_ROWS == 0 else ELEM_ROWS
    tk = ATTN_KEY_ROWS if n_tok % ATTN_KEY_ROWS == 0 else ELEM_ROWS
    for l in range(depth):
        lam_init = 0.8 - 0.6 * math.exp(-0.3 * l)

        def ffn(xs, s, sub):
            h = _normmod(xs, norm_g[l, sub:sub + 1], mod[l, :, sub], seq)
            act = _ffn_up(h, ffn_w13, l, s)
            return _resid_proj(act, w2, (l, s), xs, mod[l, :, sub, 2:3], seq, 0.5)

        xs = ffn(xs, 0, 0)

        h = _normmod(xs, norm_g[l, 1:2], mod[l, :, 1], seq)
        z, zg = _proj(h, w_in_t, gate_b[l], l)
        ya = _gmlp(z, n_tok, gmlp_norm_g[l][None], gmlp_ws[l], gmlp_bs[l].T)
        hf, hb = _mlstm(z, zg, seq // CHUNK, n_ctx // CHUNK)
        yb = _mlstm_out(hf, hb, z, mlstm_norm_g[l][None])
        qr, kr = _rope(z, cos_t, sin_t)
        ng = diff_norm_g[l][None]
        yc_lat = _attention(qr, kr, z, diff_lambda[l], ng, lam_init, 0, seq, 0, n_tok, tq, tk)
        yc_ctx = _attention(qr, kr, z, diff_lambda[l], ng, lam_init, seq, n_ctx, seq, n_ctx, n_ctx, n_ctx)
        yc = jnp.concatenate([yc_lat, yc_ctx], axis=0)
        y = _merge(ya, yb, yc, z, wbr, l)
        xs = _resid_proj(y, wo, (l,), xs, mod[l, :, 1, 2:3], seq, 1.0)

        xs = ffn(xs, 1, 2)
    return _final_norm(xs, final_g[None], seq)[None]
```
